```python
import jax, jax.numpy as jnp
from jax import lax
import numpy as np

D_MODEL = 1024
BATCH = 8
SEQ = 2048
DEPTH = 4
DEC_BATCH = 32
DEC_SEQ = 64
PAST_LEN = 4096

CHUNK = 64
N_MIXERS = 2
N_A_LAYERS = (DEPTH + 1) // 2
N_B_LAYERS = DEPTH // 2
EPS = 1e-6
NEG_INF = -1e30

A_HEADS = 8
A_HEAD_DIM = D_MODEL // A_HEADS
A_WIDTH = A_HEADS * A_HEAD_DIM
A_IN_COLS = 4 * A_WIDTH + A_HEADS
Q_BLOCK = 128

B_WIDTH = 2 * D_MODEL
B_GROUPS = 8
B_GROUP_DIM = B_WIDTH // B_GROUPS
B_IN_COLS = 3 * B_WIDTH
MLP_CHUNK = 128

kernel_name = "fox_gmlp_streaming_encoder_step"


def rmsnorm(x, g):
    xf = x.astype(jnp.float32)
    y = xf * lax.rsqrt(jnp.mean(xf * xf, axis=-1, keepdims=True) + EPS)
    return (y * g.astype(jnp.float32)).astype(x.dtype)


def fox_project(h, w_in, b_f, q_g, k_g):
    z = h @ w_in
    bsz, t = h.shape[0], h.shape[1]
    q = z[..., :A_WIDTH].reshape(bsz, t, A_HEADS, A_HEAD_DIM)
    k = z[..., A_WIDTH:2 * A_WIDTH].reshape(bsz, t, A_HEADS, A_HEAD_DIM)
    v = z[..., 2 * A_WIDTH:3 * A_WIDTH].reshape(bsz, t, A_HEADS, A_HEAD_DIM)
    gate = z[..., 3 * A_WIDTH:4 * A_WIDTH]
    f_logit = z[..., 4 * A_WIDTH:]
    q = rmsnorm(q, q_g)
    k = rmsnorm(k, k_g)
    logf = jax.nn.log_sigmoid(f_logit.astype(jnp.float32) + b_f.astype(jnp.float32))
    return q, k, v, logf, gate


def fox_attend(q, k, v, f_q, f_k, mask):
    scale = A_HEAD_DIM ** -0.5
    s = jnp.einsum('bqhd,bkhd->bhqk', q, k).astype(jnp.float32) * scale
    bias = jnp.transpose(f_q, (0, 2, 1))[..., :, None] - jnp.transpose(f_k, (0, 2, 1))[..., None, :]
    s = jnp.where(mask[None, None], s + bias, NEG_INF)
    p = jax.nn.softmax(s, axis=-1).astype(v.dtype)
    return jnp.einsum('bhqk,bkhd->bqhd', p, v)


def fox_prompt(h, w_in, b_f, q_g, k_g, w_out):
    bsz, s_len = h.shape[0], h.shape[1]
    q, k, v, logf, gate = fox_project(h, w_in, b_f, q_g, k_g)
    f_cum = jnp.cumsum(logf, axis=1)
    nb = s_len // Q_BLOCK
    q_blocks = jnp.transpose(q.reshape(bsz, nb, Q_BLOCK, A_HEADS, A_HEAD_DIM), (1, 0, 2, 3, 4))
    f_blocks = jnp.transpose(f_cum.reshape(bsz, nb, Q_BLOCK, A_HEADS), (1, 0, 2, 3))
    p_blocks = jnp.arange(s_len).reshape(nb, Q_BLOCK)
    k_pos = jnp.arange(s_len)

    def one_block(args):
        q_b, f_b, p_b = args
        mask = k_pos[None, :] <= p_b[:, None]
        return fox_attend(q_b, k, v, f_b, f_cum, mask)

    o = lax.map(one_block, (q_blocks, f_blocks, p_blocks))
    o = jnp.transpose(o, (1, 0, 2, 3, 4)).reshape(bsz, s_len, A_WIDTH)
    y = (o * jax.nn.silu(gate)) @ w_out
    return y, k, v, logf


def fox_sample(h, c_k, c_v, c_logf, w_in, b_f, q_g, k_g, w_out):
    bsz, t = h.shape[0], h.shape[1]
    p_len = c_k.shape[1]
    q, k, v, logf, gate = fox_project(h, w_in, b_f, q_g, k_g)
    f_past = jnp.cumsum(c_logf.astype(jnp.float32), axis=1)
    f_new = f_past[:, -1:, :] + jnp.cumsum(logf, axis=1)
    k_all = jnp.concatenate([c_k.astype(k.dtype), k], axis=1)
    v_all = jnp.concatenate([c_v.astype(v.dtype), v], axis=1)
    f_all = jnp.concatenate([f_past, f_new], axis=1)
    k_pos = jnp.arange(p_len + t)
    q_pos = p_len + jnp.arange(t)
    mask = k_pos[None, :] <= q_pos[:, None]
    o = fox_attend(q, k_all, v_all, f_new, f_all, mask).reshape(bsz, t, A_WIDTH)
    y = (o * jax.nn.silu(gate)) @ w_out
    return y, k, v, logf


def sgu_mask():
    c = jnp.arange(MLP_CHUNK) // CHUNK
    return c[None, :] <= c[:, None]


def gmlp_project(h, w_in, v_g):
    z = h @ w_in
    bsz, t = h.shape[0], h.shape[1]
    u = z[..., :B_WIDTH]
    v = z[..., B_WIDTH:2 * B_WIDTH].reshape(bsz, t, B_GROUPS, B_GROUP_DIM)
    gate = z[..., 2 * B_WIDTH:]
    v = rmsnorm(v, v_g)
    return u, v, gate


def gmlp_prompt(h, w_in, v_g, ws, bs, w_out):
    bsz, s_len = h.shape[0], h.shape[1]
    u, v, gate = gmlp_project(h, w_in, v_g)
    n_c = s_len // MLP_CHUNK
    vc = v.reshape(bsz, n_c, MLP_CHUNK, B_GROUPS, B_GROUP_DIM)
    w = (ws * sgu_mask()[None]).astype(v.dtype)
    mixed = jnp.einsum('gts,bcsgd->bctgd', w, vc) + jnp.transpose(bs)[None, None, :, :, None].astype(v.dtype)
    mixed = mixed.reshape(bsz, s_len, B_WIDTH)
    y = (u * mixed * jax.nn.silu(gate)) @ w_out
    return y


def gmlp_sample(h, w_in, v_g, ws, bs, w_out):
    bsz, t = h.shape[0], h.shape[1]
    u, v, gate = gmlp_project(h, w_in, v_g)
    w = (ws * sgu_mask()[None])[:, :t, :t].astype(v.dtype)
    mixed = jnp.einsum('gts,bsgd->btgd', w, v) + jnp.transpose(bs[:, :t])[None, :, :, None].astype(v.dtype)
    y = (u * mixed.reshape(bsz, t, B_WIDTH) * jax.nn.silu(gate)) @ w_out
    return y, v.reshape(bsz, t, B_WIDTH)


def setup_inputs(seed: int = 0) -> dict:
    key = jax.random.key(seed)
    ks = jax.random.split(key, 20)
    f32 = jnp.float32
    x_prompt = jax.random.normal(ks[0], (BATCH, SEQ, D_MODEL), f32)
    x_sample = jax.random.normal(ks[1], (DEC_BATCH, DEC_SEQ, D_MODEL), f32)
    cache_k = jax.random.normal(ks[2], (N_A_LAYERS, DEC_BATCH, PAST_LEN, A_HEADS, A_HEAD_DIM), f32)
    cache_v = jax.random.normal(ks[3], (N_A_LAYERS, DEC_BATCH, PAST_LEN, A_HEADS, A_HEAD_DIM), f32)
    cache_logf = jax.nn.log_sigmoid(2.5 + jax.random.normal(ks[4], (N_A_LAYERS, DEC_BATCH, PAST_LEN, A_HEADS), f32))
    norm_g = 1.0 + 0.02 * jax.random.normal(ks[5], (DEPTH, D_MODEL), f32)
    w_in_a = jax.random.normal(ks[6], (N_A_LAYERS, D_MODEL, A_IN_COLS), f32) * D_MODEL ** -0.5
    b_f = jax.random.uniform(ks[7], (N_A_LAYERS, A_HEADS), f32, minval=1.0, maxval=4.0)
    q_g = 1.0 + 0.02 * jax.random.normal(ks[8], (N_A_LAYERS, A_HEAD_DIM), f32)
    k_g = 1.0 + 0.02 * jax.random.normal(ks[9], (N_A_LAYERS, A_HEAD_DIM), f32)
    w_out_a = jax.random.normal(ks[10], (N_A_LAYERS, A_WIDTH, D_MODEL), f32) * A_WIDTH ** -0.5
    w_in_b = jax.random.normal(ks[11], (N_B_LAYERS, D_MODEL, B_IN_COLS), f32) * D_MODEL ** -0.5
    v_g = 1.0 + 0.02 * jax.random.normal(ks[12], (N_B_LAYERS, B_GROUPS, B_GROUP_DIM), f32)
    ws = jax.random.normal(ks[13], (N_B_LAYERS, B_GROUPS, MLP_CHUNK, MLP_CHUNK), f32) * MLP_CHUNK ** -0.5
    bs = 1.0 + 0.1 * jax.random.normal(ks[14], (N_B_LAYERS, B_GROUPS, MLP_CHUNK), f32)
    w_out_b = jax.random.normal(ks[15], (N_B_LAYERS, B_WIDTH, D_MODEL), f32) * B_WIDTH ** -0.5
    return {"x_prompt": x_prompt, "x_sample": x_sample,
            "cache_k": cache_k, "cache_v": cache_v, "cache_logf": cache_logf,
            "norm_g": norm_g, "w_in_a": w_in_a, "b_f": b_f, "q_g": q_g, "k_g": k_g, "w_out_a": w_out_a,
            "w_in_b": w_in_b, "v_g": v_g, "ws": ws, "bs": bs, "w_out_b": w_out_b}


def reference(x_prompt, x_sample, cache_k, cache_v, cache_logf, norm_g, w_in_a, b_f, q_g, k_g, w_out_a,
              w_in_b, v_g, ws, bs, w_out_b):
    xp, xs = x_prompt, x_sample
    kp_l, vp_l, lp_l, ks_l, vs_l, ls_l, sgu_l = [], [], [], [], [], [], []
    for i in range(DEPTH):
        hp = rmsnorm(xp, norm_g[i])
        hs = rmsnorm(xs, norm_g[i])
        j = i // N_MIXERS
        if i % N_MIXERS == 0:
            yp, kp, vp, lp = fox_prompt(hp, w_in_a[j], b_f[j], q_g[j], k_g[j], w_out_a[j])
            ys, ks_, vs_, ls_ = fox_sample(hs, cache_k[j], cache_v[j], cache_logf[j],
                                           w_in_a[j], b_f[j], q_g[j], k_g[j], w_out_a[j])
            kp_l.append(kp); vp_l.append(vp); lp_l.append(lp)
            ks_l.append(ks_); vs_l.append(vs_); ls_l.append(ls_)
        else:
            yp = gmlp_prompt(hp, w_in_b[j], v_g[j], ws[j], bs[j], w_out_b[j])
            ys, sv = gmlp_sample(hs, w_in_b[j], v_g[j], ws[j], bs[j], w_out_b[j])
            sgu_l.append(sv)
        xp = xp + yp
        xs = xs + ys
    k_prompt = jnp.stack(kp_l)
    v_prompt = jnp.stack(vp_l)
    logf_prompt = jnp.stack(lp_l)
    k_sample = jnp.stack(ks_l)
    v_sample = jnp.stack(vs_l)
    logf_sample = jnp.stack(ls_l)
    sgu_v_sample = jnp.stack(sgu_l)
    return (xp, xs, k_prompt, v_prompt, logf_prompt, k_sample, v_sample, logf_sample, sgu_v_sample)
```

```python
import functools

import jax
import jax.numpy as jnp
from jax import lax
from jax.experimental import pallas as pl
from jax.experimental.pallas import tpu as pltpu

F32 = jnp.float32
BF16 = jnp.bfloat16

EPS = 1e-6
NEG_INF = -1e30

HEADS = 8
HEAD_DIM = 128
GROUPS = 8
MLP_CHUNK = 128
STREAM_CHUNK = 64

V7X_VMEM_BYTES = 64 * 1024 * 1024
VMEM_LIMIT_BYTES = V7X_VMEM_BYTES - 8 * 1024 * 1024

ROW_TILE = 256
ATTN_BLOCK = 256
SAMPLE_KV_BLOCK = 1024
CUMSUM_BLOCK = 256


def _params(n_axes):
    return pltpu.CompilerParams(dimension_semantics=("arbitrary",) * n_axes,
                                vmem_limit_bytes=VMEM_LIMIT_BYTES)


def _const_spec(shape):
    nd = len(shape)
    return pl.BlockSpec(shape, lambda *_: (0,) * nd, pipeline_mode=pl.Buffered(1))


def _rmsnorm(x, g):
    ms = jnp.mean(x * x, axis=-1, keepdims=True)
    return x * lax.rsqrt(ms + EPS) * g


def _dot(a, b):
    return jnp.dot(a, b, preferred_element_type=F32)


def _dot_nt(a, b):
    return lax.dot_general(a, b, (((1,), (1,)), ((), ())), preferred_element_type=F32)


def _fox_inproj_kernel(x_ref, g_ref, w_ref, wf_ref, bf_ref, qg_ref, kg_ref,
                       q16_ref, k32_ref, v32_ref, k16_ref, v16_ref, sg_ref, logf_ref):
    width = HEADS * HEAD_DIM
    h = _rmsnorm(x_ref[...], g_ref[...]).astype(BF16)

    zq = _dot(h, w_ref[:, 0:width])
    for hd in range(HEADS):
        sl = slice(hd * HEAD_DIM, (hd + 1) * HEAD_DIM)
        q16_ref[:, sl] = _rmsnorm(zq[:, sl], qg_ref[...]).astype(BF16)

    zk = _dot(h, w_ref[:, width:2 * width])
    for hd in range(HEADS):
        sl = slice(hd * HEAD_DIM, (hd + 1) * HEAD_DIM)
        kn = _rmsnorm(zk[:, sl], kg_ref[...])
        k32_ref[:, sl] = kn
        k16_ref[:, sl] = kn.astype(BF16)

    zv = _dot(h, w_ref[:, 2 * width:3 * width])
    v32_ref[...] = zv
    v16_ref[...] = zv.astype(BF16)

    zg = _dot(h, w_ref[:, 3 * width:4 * width])
    sg_ref[...] = jax.nn.silu(zg)

    zf = _dot(h, wf_ref[...])
    logf_ref[...] = jax.nn.log_sigmoid(zf[:, 0:HEADS] + bf_ref[...])


def _fox_inproj(x, g, w, wf, b_f, q_g, k_g):
    n, d = x.shape
    width = HEADS * HEAD_DIM
    tm = ROW_TILE
    row = lambda cols: pl.BlockSpec((tm, cols), lambda i: (i, 0))
    out_shape = (
        jax.ShapeDtypeStruct((n, width), BF16),
        jax.ShapeDtypeStruct((n, width), F32),
        jax.ShapeDtypeStruct((n, width), F32),
        jax.ShapeDtypeStruct((n, width), BF16),
        jax.ShapeDtypeStruct((n, width), BF16),
        jax.ShapeDtypeStruct((n, width), F32),
        jax.ShapeDtypeStruct((n, HEADS), F32),
    )
    return pl.pallas_call(
        _fox_inproj_kernel,
        grid=(n // tm,),
        in_specs=[row(d), _const_spec((1, d)), _const_spec(w.shape), _const_spec(wf.shape),
                  _const_spec((1, HEADS)), _const_spec((1, HEAD_DIM)), _const_spec((1, HEAD_DIM))],
        out_specs=(row(width), row(width), row(width), row(width), row(width), row(width), row(HEADS)),
        out_shape=out_shape,
        compiler_params=_params(1),
    )(x, g.reshape(1, d), w, wf, b_f.reshape(1, HEADS), q_g.reshape(1, HEAD_DIM), k_g.reshape(1, HEAD_DIM))


def _split3(x):
    hi = x.astype(BF16)
    r1 = x - hi.astype(F32)
    mid = r1.astype(BF16)
    lo = (r1 - mid.astype(F32)).astype(BF16)
    return hi, mid, lo


def _cumsum_kernel(x_ref, o_ref, *, chained):
    _, nb, lb = x_ref.shape
    r = lax.broadcasted_iota(jnp.int32, (lb, lb), 0)
    c = lax.broadcasted_iota(jnp.int32, (lb, lb), 1)
    upper = (r <= c).astype(BF16)
    if chained:
        rb = lax.broadcasted_iota(jnp.int32, (nb, nb), 0)
        cb = lax.broadcasted_iota(jnp.int32, (nb, nb), 1)
    for hd in range(HEADS):
        hi, mid, lo = _split3(x_ref[hd])
        cs = _dot(hi, upper) + _dot(mid, upper) + _dot(lo, upper)
        if chained:
            tot = jnp.broadcast_to(cs[:, lb - 1:lb], (nb, nb))
            tot_row = jnp.sum(jnp.where(rb == cb, tot, 0.0), axis=0, keepdims=True)
            offs = jnp.sum(jnp.where(cb < rb, jnp.broadcast_to(tot_row, (nb, nb)), 0.0),
                           axis=1, keepdims=True)
            cs = cs + offs
        o_ref[hd] = cs


def _cumsum_lanes(x, chained):
    bsz, _, nb, lb = x.shape
    spec = pl.BlockSpec((None, HEADS, nb, lb), lambda b: (b, 0, 0, 0))
    return pl.pallas_call(
        functools.partial(_cumsum_kernel, chained=chained),
        grid=(bsz,),
        in_specs=[spec],
        out_specs=spec,
        out_shape=jax.ShapeDtypeStruct(x.shape, F32),
        compiler_params=_params(1),
    )(x)


def _softmax_block(q, k, v, bias, carry, mask):
    m, l, acc = carry
    s = _dot_nt(q, k) * (HEAD_DIM ** -0.5) + bias
    if mask is not None:
        s = jnp.where(mask, s, NEG_INF)
    m_new = jnp.maximum(m, jnp.max(s, axis=1, keepdims=True))
    p = jnp.exp(s - m_new)
    alpha = jnp.exp(m - m_new)
    l = alpha * l + jnp.sum(p, axis=1, keepdims=True)
    acc = alpha * acc + _dot(p.astype(BF16), v)
    return m_new, l, acc


def _row_to_col(row, eye):
    n = row.shape[1]
    return jnp.sum(jnp.where(eye, jnp.broadcast_to(row, (n, n)), 0.0), axis=1, keepdims=True)


def _prompt_attn_kernel(q_ref, k_ref, v_ref, sg_ref, f_ref, og_ref):
    t = ATTN_BLOCK
    nblk = f_ref.shape[0]
    rows = lax.broadcasted_iota(jnp.int32, (t, t), 0)
    cols = lax.broadcasted_iota(jnp.int32, (t, t), 1)
    eye = rows == cols
    causal = cols <= rows
    for qi in range(nblk):
        qs = slice(qi * t, (qi + 1) * t)
        q = q_ref[qs, :]
        f_q = _row_to_col(f_ref[qi], eye)

        def body(ki, carry, q=q, f_q=f_q):
            start = pl.multiple_of(ki * t, t)
            return _softmax_block(q, k_ref[pl.ds(start, t), :], v_ref[pl.ds(start, t), :],
                                  f_q - f_ref[ki], carry, None)

        carry = (jnp.full((t, 1), NEG_INF, F32), jnp.zeros((t, 1), F32), jnp.zeros((t, HEAD_DIM), F32))
        carry = lax.fori_loop(0, qi, body, carry)
        _, l, acc = _softmax_block(q, k_ref[qs, :], v_ref[qs, :], f_q - f_ref[qi], carry, causal)
        og_ref[qs, :] = (acc / l * sg_ref[qs, :]).astype(BF16)


def _prompt_attn(q16, k16, v16, sg, fcum, bsz, s_len):
    t = ATTN_BLOCK
    n, width = q16.shape
    blk = pl.BlockSpec((s_len, HEAD_DIM), lambda b, h: (b, h))
    return pl.pallas_call(
        _prompt_attn_kernel,
        grid=(bsz, HEADS),
        in_specs=[blk, blk, blk, blk,
                  pl.BlockSpec((None, None, s_len // t, 1, t), lambda b, h: (b, h, 0, 0, 0))],
        out_specs=blk,
        out_shape=jax.ShapeDtypeStruct((n, width), BF16),
        compiler_params=_params(2),
    )(q16, k16, v16, sg, fcum)


def _sample_attn_kernel(q_ref, ck_ref, cv_ref, kn_ref, vn_ref, sg_ref, fp_ref, fl_ref, fn_ref,
                        og_ref, m_ref, l_ref, acc_ref, fq_ref):
    ki = pl.program_id(1)
    t = q_ref.shape[0]
    rows = lax.broadcasted_iota(jnp.int32, (t, t), 0)
    cols = lax.broadcasted_iota(jnp.int32, (t, t), 1)

    @pl.when(ki == 0)
    def _():
        m_ref[...] = jnp.full(m_ref.shape, NEG_INF, F32)
        l_ref[...] = jnp.zeros(l_ref.shape, F32)
        acc_ref[...] = jnp.zeros(acc_ref.shape, F32)
        f_new = fl_ref[...] + fn_ref[...]
        for hd in range(HEADS):
            fq_ref[hd] = _row_to_col(f_new[hd:hd + 1, :], rows == cols)

    for hd in range(HEADS):
        sl = slice(hd * HEAD_DIM, (hd + 1) * HEAD_DIM)
        carry = (m_ref[hd], l_ref[hd], acc_ref[hd])
        m, l, acc = _softmax_block(q_ref[:, sl], ck_ref[:, sl].astype(BF16), cv_ref[:, sl].astype(BF16),
                                   fq_ref[hd] - fp_ref[hd:hd + 1, :], carry, None)
        m_ref[hd] = m
        l_ref[hd] = l
        acc_ref[hd] = acc

    @pl.when(ki == pl.num_programs(1) - 1)
    def _():
        f_new = fl_ref[...] + fn_ref[...]
        for hd in range(HEADS):
            sl = slice(hd * HEAD_DIM, (hd + 1) * HEAD_DIM)
            carry = (m_ref[hd], l_ref[hd], acc_ref[hd])
            _, l, acc = _softmax_block(q_ref[:, sl], kn_ref[:, sl], vn_ref[:, sl],
                                       fq_ref[hd] - f_new[hd:hd + 1, :], carry, cols <= rows)
            og_ref[:, sl] = (acc / l * sg_ref[:, sl]).astype(BF16)


def _sample_attn(q16, cache_k, cache_v, k16, v16, sg, f_past, f_last, f_new, bsz, t):
    n, width = q16.shape
    p_len = cache_k.shape[1]
    tk = SAMPLE_KV_BLOCK
    new = pl.BlockSpec((t, width), lambda b, k: (b, 0))
    past = pl.BlockSpec((None, tk, width), lambda b, k: (b, k, 0))
    return pl.pallas_call(
        _sample_attn_kernel,
        grid=(bsz, p_len // tk),
        in_specs=[new, past, past, new, new, new,
                  pl.BlockSpec((None, HEADS, tk), lambda b, k: (b, 0, k)),
                  pl.BlockSpec((None, HEADS, 1), lambda b, k: (b, 0, 0)),
                  pl.BlockSpec((None, HEADS, t), lambda b, k: (b, 0, 0))],
        out_specs=new,
        out_shape=jax.ShapeDtypeStruct((n, width), BF16),
        scratch_shapes=[pltpu.VMEM((HEADS, t, 1), F32), pltpu.VMEM((HEADS, t, 1), F32),
                        pltpu.VMEM((HEADS, t, HEAD_DIM), F32), pltpu.VMEM((HEADS, t, 1), F32)],
        compiler_params=_params(2),
    )(q16, cache_k, cache_v, k16, v16, sg, f_past, f_last, f_new)


def _outproj_kernel(x_ref, a_ref, w_ref, o_ref):
    o_ref[...] = x_ref[...] + _dot(a_ref[...], w_ref[...])


def _outproj(x, a, w):
    n, d = x.shape
    tm = ROW_TILE
    return pl.pallas_call(
        _outproj_kernel,
        grid=(n // tm,),
        in_specs=[pl.BlockSpec((tm, d), lambda i: (i, 0)),
                  pl.BlockSpec((tm, a.shape[1]), lambda i: (i, 0)),
                  _const_spec(w.shape)],
        out_specs=pl.BlockSpec((tm, d), lambda i: (i, 0)),
        out_shape=jax.ShapeDtypeStruct((n, d), F32),
        compiler_params=_params(1),
    )(x, a, w)


def _gmlp_kernel(x_ref, g_ref, win_ref, vg_ref, mix_ref, bias_ref, wout_ref, *rest, emit_v):
    if emit_v:
        o_ref, vn_ref, a_ref = rest
    else:
        o_ref, a_ref = rest
    tm = x_ref.shape[0]
    bw = wout_ref.shape[0]
    gd = bw // GROUPS
    x = x_ref[...]
    h = _rmsnorm(x, g_ref[...]).astype(BF16)
    for gi in range(GROUPS):
        sl = slice(gi * gd, (gi + 1) * gd)
        v = _dot(h, win_ref[:, bw + gi * gd:bw + (gi + 1) * gd])
        vn = _rmsnorm(v, vg_ref[:, sl])
        if emit_v:
            vn_ref[:, sl] = vn
        vn16 = vn.astype(BF16)
        mixed = jnp.concatenate(
            [_dot(mix_ref[gi], vn16[ci * MLP_CHUNK:(ci + 1) * MLP_CHUNK, :]) + bias_ref[:, gi:gi + 1]
             for ci in range(tm // MLP_CHUNK)], axis=0)
        u = _dot(h, win_ref[:, sl])
        gate = _dot(h, win_ref[:, 2 * bw + gi * gd:2 * bw + (gi + 1) * gd])
        a_ref[:, sl] = (u * mixed * jax.nn.silu(gate)).astype(BF16)
    o_ref[...] = x + _dot(a_ref[...], wout_ref[...])


def _gmlp_layer(x, g, win, vg, mix, bias, wout, emit_v):
    n, d = x.shape
    bw = wout.shape[0]
    tm = ROW_TILE
    row = lambda cols: pl.BlockSpec((tm, cols), lambda i: (i, 0))
    out_shape = [jax.ShapeDtypeStruct((n, d), F32)]
    out_specs = [row(d)]
    if emit_v:
        out_shape.append(jax.ShapeDtypeStruct((n, bw), F32))
        out_specs.append(row(bw))
    res = pl.pallas_call(
        functools.partial(_gmlp_kernel, emit_v=emit_v),
        grid=(n // tm,),
        in_specs=[row(d), _const_spec((1, d)), _const_spec(win.shape), _const_spec((1, bw)),
                  _const_spec(mix.shape), _const_spec(bias.shape), _const_spec(wout.shape)],
        out_specs=tuple(out_specs),
        out_shape=tuple(out_shape),
        scratch_shapes=[pltpu.VMEM((tm, bw), BF16)],
        compiler_params=_params(1),
    )(x, g.reshape(1, d), win, vg.reshape(1, bw), mix, bias, wout)
    return res if emit_v else res[0]


def _sgu_mask():
    c = jnp.arange(MLP_CHUNK) // STREAM_CHUNK
    return c[None, :] <= c[:, None]


def kernel(x_prompt, x_sample, cache_k, cache_v, cache_logf, norm_g, w_in_a, b_f, q_g, k_g, w_out_a,
           w_in_b, v_g, ws, bs, w_out_b):
    bsz, s_len, d = x_prompt.shape
    dbsz, t_new, _ = x_sample.shape
    p_len = cache_k.shape[2]
    width = HEADS * HEAD_DIM
    depth = norm_g.shape[0]
    assert s_len % ATTN_BLOCK == 0 and s_len % CUMSUM_BLOCK == 0 and p_len % CUMSUM_BLOCK == 0
    assert p_len % SAMPLE_KV_BLOCK == 0 and MLP_CHUNK == 2 * t_new and ROW_TILE % MLP_CHUNK == 0

    xp = x_prompt.reshape(bsz * s_len, d)
    xs = x_sample.reshape(dbsz * t_new, d)
    kp_l, vp_l, lp_l, ks_l, vs_l, ls_l, sgu_l = [], [], [], [], [], [], []
    for i in range(depth):
        j = i // 2
        if i % 2 == 0:
            w = w_in_a[j][:, :4 * width].astype(BF16)
            wf = jnp.pad(w_in_a[j][:, 4 * width:], ((0, 0), (0, HEAD_DIM - HEADS))).astype(BF16)
            wo = w_out_a[j].astype(BF16)

            q16, k32, v32, k16, v16, sg, logf = _fox_inproj(xp, norm_g[i], w, wf, b_f[j], q_g[j], k_g[j])
            nb = s_len // CUMSUM_BLOCK
            lt = jnp.transpose(logf.reshape(bsz, s_len, HEADS), (0, 2, 1))
            fcum = _cumsum_lanes(lt.reshape(bsz, HEADS, nb, CUMSUM_BLOCK), chained=True)
            fcum = fcum.reshape(bsz, HEADS, s_len // ATTN_BLOCK, 1, ATTN_BLOCK)
            og = _prompt_attn(q16, k16, v16, sg, fcum, bsz, s_len)
            xp = _outproj(xp, og, wo)
            kp_l.append(k32.reshape(bsz, s_len, HEADS, HEAD_DIM))
            vp_l.append(v32.reshape(bsz, s_len, HEADS, HEAD_DIM))
            lp_l.append(logf.reshape(bsz, s_len, HEADS))

            q16, k32, v32, k16, v16, sg, logf = _fox_inproj(xs, norm_g[i], w, wf, b_f[j], q_g[j], k_g[j])
            nbp = p_len // CUMSUM_BLOCK
            cl = jnp.transpose(cache_logf[j], (0, 2, 1)).reshape(dbsz, HEADS, nbp, CUMSUM_BLOCK)
            f_past = _cumsum_lanes(cl, chained=True).reshape(dbsz, HEADS, p_len)
            lt = jnp.transpose(logf.reshape(dbsz, t_new, HEADS), (0, 2, 1))
            f_new = _cumsum_lanes(lt.reshape(1, HEADS, dbsz, t_new), chained=False)
            f_new = f_new.reshape(dbsz, HEADS, t_new)
            og = _sample_attn(q16, cache_k[j].reshape(dbsz, p_len, width), cache_v[j].reshape(dbsz, p_len, width),
                              k16, v16, sg, f_past, f_past[:, :, p_len - 1:], f_new, dbsz, t_new)
            xs = _outproj(xs, og, wo)
            ks_l.append(k32.reshape(dbsz, t_new, HEADS, HEAD_DIM))
            vs_l.append(v32.reshape(dbsz, t_new, HEADS, HEAD_DIM))
            ls_l.append(logf.reshape(dbsz, t_new, HEADS))
        else:
            win = w_in_b[j].astype(BF16)
            wout = w_out_b[j].astype(BF16)
            bw = wout.shape[0]
            mix_p = (ws[j] * _sgu_mask()[None]).astype(BF16)
            bias_p = jnp.transpose(bs[j])
            xp = _gmlp_layer(xp, norm_g[i], win, v_g[j], mix_p, bias_p, wout, emit_v=False)
            a = (ws[j] * _sgu_mask()[None])[:, :t_new, :t_new]
            z = jnp.zeros_like(a)
            mix_s = jnp.concatenate([jnp.concatenate([a, z], axis=2),
                                     jnp.concatenate([z, a], axis=2)], axis=1).astype(BF16)
            bias_s = jnp.transpose(jnp.concatenate([bs[j][:, :t_new], bs[j][:, :t_new]], axis=1))
            xs, sv = _gmlp_layer(xs, norm_g[i], win, v_g[j], mix_s, bias_s, wout, emit_v=True)
            sgu_l.append(sv.reshape(dbsz, t_new, bw))
    return (xp.reshape(bsz, s_len, d), xs.reshape(dbsz, t_new, d),
            jnp.stack(kp_l), jnp.stack(vp_l), jnp.stack(lp_l),
            jnp.stack(ks_l), jnp.stack(vs_l), jnp.stack(ls_l), jnp.stack(sgu_l))
```

```python
import functools

import jax
import jax.numpy as jnp
from jax import lax
from jax.experimental import pallas as pl
from jax.experimental.pallas import tpu as pltpu

F32 = jnp.float32
BF16 = jnp.bfloat16

EPS = 1e-6
NEG_INF = -1e30

HEADS = 8
HEAD_DIM = 128
GROUPS = 8
MLP_CHUNK = 128
STREAM_CHUNK = 64

V7X_VMEM_BYTES = 64 * 1024 * 1024
VMEM_LIMIT_BYTES = V7X_VMEM_BYTES - 8 * 1024 * 1024

ROW_TILE = 256
ATTN_BLOCK = 512
SAMPLE_KV_BLOCK = 1024
CUMSUM_BLOCK = 256


def _params(n_axes):
    return pltpu.CompilerParams(dimension_semantics=("arbitrary",) * n_axes,
                                vmem_limit_bytes=VMEM_LIMIT_BYTES)


def _const_spec(shape):
    nd = len(shape)
    return pl.BlockSpec(shape, lambda *_: (0,) * nd, pipeline_mode=pl.Buffered(1))


def _rmsnorm(x, g):
    ms = jnp.mean(x * x, axis=-1, keepdims=True)
    return x * lax.rsqrt(ms + EPS) * g


def _dot(a, b):
    return jnp.dot(a, b, preferred_element_type=F32)


def _dot_nt(a, b):
    return lax.dot_general(a, b, (((1,), (1,)), ((), ())), preferred_element_type=F32)


def _fox_inproj_kernel(x_ref, g_ref, w_ref, wf_ref, bf_ref, qg_ref, kg_ref, *rest):
    q16_ref, k32_ref, v32_ref, k16_ref, v16_ref, sg_ref, logf_ref = rest[-7:]
    tm = x_ref.shape[0]
    width = HEADS * HEAD_DIM
    h = _rmsnorm(x_ref[...], g_ref[...]).astype(BF16)

    zq = _dot(h, w_ref[:, 0:width])
    for hd in range(HEADS):
        sl = slice(hd * HEAD_DIM, (hd + 1) * HEAD_DIM)
        q16_ref[:, sl] = _rmsnorm(zq[:, sl], qg_ref[...]).astype(BF16)

    zk = _dot(h, w_ref[:, width:2 * width])
    for hd in range(HEADS):
        sl = slice(hd * HEAD_DIM, (hd + 1) * HEAD_DIM)
        kn = _rmsnorm(zk[:, sl], kg_ref[...])
        k32_ref[pl.ds(hd, tm, stride=HEADS), :] = kn
        k16_ref[:, sl] = kn.astype(BF16)

    zv = _dot(h, w_ref[:, 2 * width:3 * width])
    for hd in range(HEADS):
        sl = slice(hd * HEAD_DIM, (hd + 1) * HEAD_DIM)
        v32_ref[pl.ds(hd, tm, stride=HEADS), :] = zv[:, sl]
    v16_ref[...] = zv.astype(BF16)

    zg = _dot(h, w_ref[:, 3 * width:4 * width])
    sg_ref[...] = jax.nn.silu(zg)

    zf = _dot(h, wf_ref[...])
    logf_ref[...] = jax.nn.log_sigmoid(zf[:, 0:HEADS] + bf_ref[...])


def _fox_inproj(x, g, w, wf, b_f, q_g, k_g, layer, n_layers, kv_prev):
    n, d = x.shape
    width = HEADS * HEAD_DIM
    tm = ROW_TILE
    row = lambda cols: pl.BlockSpec((tm, cols), lambda i: (i, 0))
    kv_spec = pl.BlockSpec((None, tm * HEADS, HEAD_DIM), lambda i: (layer, i, 0))
    kv_shape = jax.ShapeDtypeStruct((n_layers, n * HEADS, HEAD_DIM), F32)
    out_shape = (
        jax.ShapeDtypeStruct((n, width), BF16),
        kv_shape,
        kv_shape,
        jax.ShapeDtypeStruct((n, width), BF16),
        jax.ShapeDtypeStruct((n, width), BF16),
        jax.ShapeDtypeStruct((n, width), F32),
        jax.ShapeDtypeStruct((n, HEADS), F32),
    )
    in_specs = [row(d), _const_spec((1, d)), _const_spec(w.shape), _const_spec(wf.shape),
                _const_spec((1, HEADS)), _const_spec((1, HEAD_DIM)), _const_spec((1, HEAD_DIM))]
    args = [x, g.reshape(1, d), w, wf, b_f.reshape(1, HEADS), q_g.reshape(1, HEAD_DIM), k_g.reshape(1, HEAD_DIM)]
    aliases = {}
    if kv_prev is not None:
        in_specs += [pl.BlockSpec(memory_space=pl.ANY)] * 2
        aliases = {len(args): 1, len(args) + 1: 2}
        args += list(kv_prev)
    return pl.pallas_call(
        _fox_inproj_kernel,
        grid=(n // tm,),
        in_specs=in_specs,
        out_specs=(row(width), kv_spec, kv_spec, row(width), row(width), row(width), row(HEADS)),
        out_shape=out_shape,
        input_output_aliases=aliases,
        compiler_params=_params(1),
        name="fox_inproj",
    )(*args)


def _split3(x):
    hi = x.astype(BF16)
    r1 = x - hi.astype(F32)
    mid = r1.astype(BF16)
    lo = (r1 - mid.astype(F32)).astype(BF16)
    return hi, mid, lo


def _cumsum_kernel(x_ref, o_ref, *, chained):
    _, nb, lb = x_ref.shape
    r = lax.broadcasted_iota(jnp.int32, (lb, lb), 0)
    c = lax.broadcasted_iota(jnp.int32, (lb, lb), 1)
    upper = (r <= c).astype(BF16)
    if chained:
        rb = lax.broadcasted_iota(jnp.int32, (nb, nb), 0)
        cb = lax.broadcasted_iota(jnp.int32, (nb, nb), 1)
    for hd in range(HEADS):
        hi, mid, lo = _split3(x_ref[hd])
        cs = _dot(hi, upper) + _dot(mid, upper) + _dot(lo, upper)
        if chained:
            tot = jnp.broadcast_to(cs[:, lb - 1:lb], (nb, nb))
            tot_row = jnp.sum(jnp.where(rb == cb, tot, 0.0), axis=0, keepdims=True)
            offs = jnp.sum(jnp.where(cb < rb, jnp.broadcast_to(tot_row, (nb, nb)), 0.0),
                           axis=1, keepdims=True)
            cs = cs + offs
        o_ref[hd] = cs


def _cumsum_lanes(x, chained):
    bsz, _, nb, lb = x.shape
    spec = pl.BlockSpec((None, HEADS, nb, lb), lambda b: (b, 0, 0, 0))
    return pl.pallas_call(
        functools.partial(_cumsum_kernel, chained=chained),
        grid=(bsz,),
        in_specs=[spec],
        out_specs=spec,
        out_shape=jax.ShapeDtypeStruct(x.shape, F32),
        compiler_params=_params(1),
        name="logf_cumsum",
    )(x)


def _softmax_block(q, k, v, bias, carry, mask):
    m, l, acc = carry
    s = _dot_nt(q, k) * (HEAD_DIM ** -0.5) + bias
    if mask is not None:
        s = jnp.where(mask, s, NEG_INF)
    m_new = jnp.maximum(m, jnp.max(s, axis=1, keepdims=True))
    p = jnp.exp(s - m_new)
    alpha = jnp.exp(m - m_new)
    l = alpha * l + jnp.sum(p, axis=1, keepdims=True)
    acc = alpha * acc + _dot(p.astype(BF16), v)
    return m_new, l, acc


def _row_to_col(row, eye):
    n = row.shape[1]
    return jnp.sum(jnp.where(eye, jnp.broadcast_to(row, (n, n)), 0.0), axis=1, keepdims=True)


def _prompt_attn_kernel(q_ref, k_ref, v_ref, sg_ref, f_ref, og_ref):
    t = ATTN_BLOCK
    nblk = f_ref.shape[0]
    rows = lax.broadcasted_iota(jnp.int32, (t, t), 0)
    cols = lax.broadcasted_iota(jnp.int32, (t, t), 1)
    eye = rows == cols
    causal = cols <= rows
    for qi in range(nblk):
        qs = slice(qi * t, (qi + 1) * t)
        q = q_ref[qs, :]
        f_q = _row_to_col(f_ref[qi], eye)

        def body(ki, carry, q=q, f_q=f_q):
            start = pl.multiple_of(ki * t, t)
            return _softmax_block(q, k_ref[pl.ds(start, t), :], v_ref[pl.ds(start, t), :],
                                  f_q - f_ref[ki], carry, None)

        carry = (jnp.full((t, 1), NEG_INF, F32), jnp.zeros((t, 1), F32), jnp.zeros((t, HEAD_DIM), F32))
        carry = lax.fori_loop(0, qi, body, carry)
        _, l, acc = _softmax_block(q, k_ref[qs, :], v_ref[qs, :], f_q - f_ref[qi], carry, causal)
        og_ref[qs, :] = (acc / l * sg_ref[qs, :]).astype(BF16)


def _prompt_attn(q16, k16, v16, sg, fcum, bsz, s_len):
    t = ATTN_BLOCK
    n, width = q16.shape
    blk = pl.BlockSpec((s_len, HEAD_DIM), lambda b, h: (b, h))
    return pl.pallas_call(
        _prompt_attn_kernel,
        grid=(bsz, HEADS),
        in_specs=[blk, blk, blk, blk,
                  pl.BlockSpec((None, None, s_len // t, 1, t), lambda b, h: (b, h, 0, 0, 0))],
        out_specs=blk,
        out_shape=jax.ShapeDtypeStruct((n, width), BF16),
        compiler_params=_params(2),
        name="prompt_attn",
    )(q16, k16, v16, sg, fcum)


def _sample_attn_kernel(q_ref, ck_ref, cv_ref, kn_ref, vn_ref, sg_ref, fp_ref, fl_ref, fn_ref,
                        og_ref, m_ref, l_ref, acc_ref, fq_ref):
    ki = pl.program_id(1)
    t = q_ref.shape[0]
    tk = fp_ref.shape[1]
    rows = lax.broadcasted_iota(jnp.int32, (t, t), 0)
    cols = lax.broadcasted_iota(jnp.int32, (t, t), 1)

    @pl.when(ki == 0)
    def _():
        m_ref[...] = jnp.full(m_ref.shape, NEG_INF, F32)
        l_ref[...] = jnp.zeros(l_ref.shape, F32)
        acc_ref[...] = jnp.zeros(acc_ref.shape, F32)
        f_new = fl_ref[...] + fn_ref[...]
        for hd in range(HEADS):
            fq_ref[hd] = _row_to_col(f_new[hd:hd + 1, :], rows == cols)

    for hd in range(HEADS):
        sl = slice(hd * HEAD_DIM, (hd + 1) * HEAD_DIM)
        carry = (m_ref[hd], l_ref[hd], acc_ref[hd])
        k = ck_ref[pl.ds(hd, tk, stride=HEADS), :].astype(BF16)
        v = cv_ref[pl.ds(hd, tk, stride=HEADS), :].astype(BF16)
        m, l, acc = _softmax_block(q_ref[:, sl], k, v, fq_ref[hd] - fp_ref[hd:hd + 1, :], carry, None)
        m_ref[hd] = m
        l_ref[hd] = l
        acc_ref[hd] = acc

    @pl.when(ki == pl.num_programs(1) - 1)
    def _():
        f_new = fl_ref[...] + fn_ref[...]
        for hd in range(HEADS):
            sl = slice(hd * HEAD_DIM, (hd + 1) * HEAD_DIM)
            carry = (m_ref[hd], l_ref[hd], acc_ref[hd])
            _, l, acc = _softmax_block(q_ref[:, sl], kn_ref[:, sl], vn_ref[:, sl],
                                       fq_ref[hd] - f_new[hd:hd + 1, :], carry, cols <= rows)
            og_ref[:, sl] = (acc / l * sg_ref[:, sl]).astype(BF16)


def _sample_attn(q16, cache_k, cache_v, layer, k16, v16, sg, f_past, f_last, f_new, bsz, t):
    n, width = q16.shape
    p_len = f_past.shape[2]
    tk = SAMPLE_KV_BLOCK
    new = pl.BlockSpec((t, width), lambda b, k: (b, 0))
    past = pl.BlockSpec((None, None, tk * HEADS, HEAD_DIM), lambda b, k: (layer, b, k, 0))
    return pl.pallas_call(
        _sample_attn_kernel,
        grid=(bsz, p_len // tk),
        in_specs=[new, past, past, new, new, new,
                  pl.BlockSpec((None, HEADS, tk), lambda b, k: (b, 0, k)),
                  pl.BlockSpec((None, HEADS, 1), lambda b, k: (b, 0, 0)),
                  pl.BlockSpec((None, HEADS, t), lambda b, k: (b, 0, 0))],
        out_specs=new,
        out_shape=jax.ShapeDtypeStruct((n, width), BF16),
        scratch_shapes=[pltpu.VMEM((HEADS, t, 1), F32), pltpu.VMEM((HEADS, t, 1), F32),
                        pltpu.VMEM((HEADS, t, HEAD_DIM), F32), pltpu.VMEM((HEADS, t, 1), F32)],
        compiler_params=_params(2),
        name="sample_attn",
    )(q16, cache_k, cache_v, k16, v16, sg, f_past, f_last, f_new)


def _outproj_kernel(x_ref, a_ref, w_ref, o_ref):
    o_ref[...] = x_ref[...] + _dot(a_ref[...], w_ref[...])


def _outproj(x, a, w):
    n, d = x.shape
    tm = ROW_TILE
    return pl.pallas_call(
        _outproj_kernel,
        grid=(n // tm,),
        in_specs=[pl.BlockSpec((tm, d), lambda i: (i, 0)),
                  pl.BlockSpec((tm, a.shape[1]), lambda i: (i, 0)),
                  _const_spec(w.shape)],
        out_specs=pl.BlockSpec((tm, d), lambda i: (i, 0)),
        out_shape=jax.ShapeDtypeStruct((n, d), F32),
        compiler_params=_params(1),
        name="outproj",
    )(x, a, w)


def _gmlp_kernel(x_ref, g_ref, win_ref, vg_ref, mix_ref, bias_ref, wout_ref, *rest, emit_v):
    if emit_v:
        o_ref, vn_ref, a_ref = rest
    else:
        o_ref, a_ref = rest
    tm = x_ref.shape[0]
    bw = wout_ref.shape[0]
    gd = bw // GROUPS
    x = x_ref[...]
    h = _rmsnorm(x, g_ref[...]).astype(BF16)
    for gi in range(GROUPS):
        sl = slice(gi * gd, (gi + 1) * gd)
        v = _dot(h, win_ref[:, bw + gi * gd:bw + (gi + 1) * gd])
        vn = _rmsnorm(v, vg_ref[:, sl])
        if emit_v:
            vn_ref[:, sl] = vn
        vn16 = vn.astype(BF16)
        mixed = jnp.concatenate(
            [_dot(mix_ref[gi], vn16[ci * MLP_CHUNK:(ci + 1) * MLP_CHUNK, :]) + bias_ref[:, gi:gi + 1]
             for ci in range(tm // MLP_CHUNK)], axis=0)
        u = _dot(h, win_ref[:, sl])
        gate = _dot(h, win_ref[:, 2 * bw + gi * gd:2 * bw + (gi + 1) * gd])
        a_ref[:, sl] = (u * mixed * jax.nn.silu(gate)).astype(BF16)
    o_ref[...] = x + _dot(a_ref[...], wout_ref[...])


def _gmlp_layer(x, g, win, vg, mix, bias, wout, emit_v):
    n, d = x.shape
    bw = wout.shape[0]
    tm = ROW_TILE
    row = lambda cols: pl.BlockSpec((tm, cols), lambda i: (i, 0))
    out_shape = [jax.ShapeDtypeStruct((n, d), F32)]
    out_specs = [row(d)]
    if emit_v:
        out_shape.append(jax.ShapeDtypeStruct((n, bw), F32))
        out_specs.append(row(bw))
    res = pl.pallas_call(
        functools.partial(_gmlp_kernel, emit_v=emit_v),
        grid=(n // tm,),
        in_specs=[row(d), _const_spec((1, d)), _const_spec(win.shape), _const_spec((1, bw)),
                  _const_spec(mix.shape), _const_spec(bias.shape), _const_spec(wout.shape)],
        out_specs=tuple(out_specs),
        out_shape=tuple(out_shape),
        scratch_shapes=[pltpu.VMEM((tm, bw), BF16)],
        compiler_params=_params(1),
        name="gmlp_layer",
    )(x, g.reshape(1, d), win, vg.reshape(1, bw), mix, bias, wout)
    return res if emit_v else res[0]


def _sgu_mask():
    c = jnp.arange(MLP_CHUNK) // STREAM_CHUNK
    return c[None, :] <= c[:, None]


def kernel(x_prompt, x_sample, cache_k, cache_v, cache_logf, norm_g, w_in_a, b_f, q_g, k_g, w_out_a,
           w_in_b, v_g, ws, bs, w_out_b):
    bsz, s_len, d = x_prompt.shape
    dbsz, t_new, _ = x_sample.shape
    p_len = cache_k.shape[2]
    width = HEADS * HEAD_DIM
    depth = norm_g.shape[0]
    assert s_len % ATTN_BLOCK == 0 and s_len % CUMSUM_BLOCK == 0 and p_len % CUMSUM_BLOCK == 0
    assert p_len % SAMPLE_KV_BLOCK == 0 and MLP_CHUNK == 2 * t_new and ROW_TILE % MLP_CHUNK == 0

    xp = x_prompt.reshape(bsz * s_len, d)
    xs = x_sample.reshape(dbsz * t_new, d)
    n_a = w_in_a.shape[0]
    ck = cache_k.reshape(n_a, dbsz, p_len * HEADS, HEAD_DIM)
    cv = cache_v.reshape(n_a, dbsz, p_len * HEADS, HEAD_DIM)
    kv_p, kv_s = None, None
    lp_l, ls_l, sgu_l = [], [], []
    for i in range(depth):
        j = i // 2
        if i % 2 == 0:
            w = w_in_a[j][:, :4 * width].astype(BF16)
            wf = jnp.pad(w_in_a[j][:, 4 * width:], ((0, 0), (0, HEAD_DIM - HEADS))).astype(BF16)
            wo = w_out_a[j].astype(BF16)

            q16, k_all, v_all, k16, v16, sg, logf = _fox_inproj(xp, norm_g[i], w, wf, b_f[j], q_g[j], k_g[j],
                                                                j, n_a, kv_p)
            kv_p = (k_all, v_all)
            nb = s_len // CUMSUM_BLOCK
            lt = jnp.transpose(logf.reshape(bsz, s_len, HEADS), (0, 2, 1))
            fcum = _cumsum_lanes(lt.reshape(bsz, HEADS, nb, CUMSUM_BLOCK), chained=True)
            fcum = fcum.reshape(bsz, HEADS, s_len // ATTN_BLOCK, 1, ATTN_BLOCK)
            og = _prompt_attn(q16, k16, v16, sg, fcum, bsz, s_len)
            xp = _outproj(xp, og, wo)
            lp_l.append(logf.reshape(bsz, s_len, HEADS))

            q16, k_all, v_all, k16, v16, sg, logf = _fox_inproj(xs, norm_g[i], w, wf, b_f[j], q_g[j], k_g[j],
                                                                j, n_a, kv_s)
            kv_s = (k_all, v_all)
            nbp = p_len // CUMSUM_BLOCK
            cl = jnp.transpose(cache_logf[j], (0, 2, 1)).reshape(dbsz, HEADS, nbp, CUMSUM_BLOCK)
            f_past = _cumsum_lanes(cl, chained=True).reshape(dbsz, HEADS, p_len)
            lt = jnp.transpose(logf.reshape(dbsz, t_new, HEADS), (0, 2, 1))
            f_new = _cumsum_lanes(lt.reshape(1, HEADS, dbsz, t_new), chained=False)
            f_new = f_new.reshape(dbsz, HEADS, t_new)
            og = _sample_attn(q16, ck, cv, j, k16, v16, sg, f_past, f_past[:, :, p_len - 1:], f_new, dbsz, t_new)
            xs = _outproj(xs, og, wo)
            ls_l.append(logf.reshape(dbsz, t_new, HEADS))
        else:
            win = w_in_b[j].astype(BF16)
            wout = w_out_b[j].astype(BF16)
            bw = wout.shape[0]
            mix_p = (ws[j] * _sgu_mask()[None]).astype(BF16)
            bias_p = jnp.transpose(bs[j])
            xp = _gmlp_layer(xp, norm_g[i], win, v_g[j], mix_p, bias_p, wout, emit_v=False)
            a = (ws[j] * _sgu_mask()[None])[:, :t_new, :t_new]
            z = jnp.zeros_like(a)
            mix_s = jnp.concatenate([jnp.concatenate([a, z], axis=2),
                                     jnp.concatenate([z, a], axis=2)], axis=1).astype(BF16)
            bias_s = jnp.transpose(jnp.concatenate([bs[j][:, :t_new], bs[j][:, :t_new]], axis=1))
            xs, sv = _gmlp_layer(xs, norm_g[i], win, v_g[j], mix_s, bias_s, wout, emit_v=True)
            sgu_l.append(sv.reshape(dbsz, t_new, bw))
    kv5 = lambda a, nb_, t_: a.reshape(n_a, nb_, t_, HEADS, HEAD_DIM)
    return (xp.reshape(bsz, s_len, d), xs.reshape(dbsz, t_new, d),
            kv5(kv_p[0], bsz, s_len), kv5(kv_p[1], bsz, s_len), jnp.stack(lp_l),
            kv5(kv_s[0], dbsz, t_new), kv5(kv_s[1], dbsz, t_new), jnp.stack(ls_l), jnp.stack(sgu_l))
```

```python
import functools

import jax
import jax.numpy as jnp
from jax import lax
from jax.experimental import pallas as pl
from jax.experimental.pallas import tpu as pltpu

F32 = jnp.float32
BF16 = jnp.bfloat16

EPS = 1e-6
NEG_INF = -1e30

HEADS = 8
HEAD_DIM = 128
GROUPS = 8
MLP_CHUNK = 128
STREAM_CHUNK = 64

V7X_VMEM_BYTES = 64 * 1024 * 1024
VMEM_LIMIT_BYTES = V7X_VMEM_BYTES - 8 * 1024 * 1024

ROW_TILE = 512
GMLP_ROW_TILE = 512
OUTPROJ_ROW_TILE = 1024
ATTN_BLOCK = 512
SAMPLE_KV_BLOCK = 1024
CUMSUM_BLOCK = 256


def _params(n_axes):
    return pltpu.CompilerParams(dimension_semantics=("arbitrary",) * n_axes,
                                vmem_limit_bytes=VMEM_LIMIT_BYTES)


def _const_spec(shape):
    nd = len(shape)
    return pl.BlockSpec(shape, lambda *_: (0,) * nd, pipeline_mode=pl.Buffered(1))


def _rmsnorm(x, g):
    ms = jnp.mean(x * x, axis=-1, keepdims=True)
    return x * lax.rsqrt(ms + EPS) * g


def _dot(a, b):
    return jnp.dot(a, b, preferred_element_type=F32)


def _dot_nt(a, b):
    return lax.dot_general(a, b, (((1,), (1,)), ((), ())), preferred_element_type=F32)


def _fox_inproj_kernel(x_ref, g_ref, w_ref, wf_ref, bf_ref, qg_ref, kg_ref, *rest):
    q16_ref, k32_ref, v32_ref, k16_ref, v16_ref, sg_ref, logf_ref = rest[-7:]
    tm = x_ref.shape[0]
    width = HEADS * HEAD_DIM
    h = _rmsnorm(x_ref[...], g_ref[...]).astype(BF16)

    zq = _dot(h, w_ref[:, 0:width])
    for hd in range(HEADS):
        sl = slice(hd * HEAD_DIM, (hd + 1) * HEAD_DIM)
        q16_ref[:, sl] = _rmsnorm(zq[:, sl], qg_ref[...]).astype(BF16)

    zk = _dot(h, w_ref[:, width:2 * width])
    for hd in range(HEADS):
        sl = slice(hd * HEAD_DIM, (hd + 1) * HEAD_DIM)
        kn = _rmsnorm(zk[:, sl], kg_ref[...])
        k32_ref[pl.ds(hd, tm, stride=HEADS), :] = kn
        k16_ref[:, sl] = kn.astype(BF16)

    zv = _dot(h, w_ref[:, 2 * width:3 * width])
    for hd in range(HEADS):
        sl = slice(hd * HEAD_DIM, (hd + 1) * HEAD_DIM)
        v32_ref[pl.ds(hd, tm, stride=HEADS), :] = zv[:, sl]
    v16_ref[...] = zv.astype(BF16)

    zg = _dot(h, w_ref[:, 3 * width:4 * width])
    sg_ref[...] = jax.nn.silu(zg)

    zf = _dot(h, wf_ref[...])
    logf_ref[...] = jax.nn.log_sigmoid(zf[:, 0:HEADS] + bf_ref[...])


def _fox_inproj(x, g, w, wf, b_f, q_g, k_g, layer, n_layers, kv_prev):
    n, d = x.shape
    width = HEADS * HEAD_DIM
    tm = ROW_TILE
    row = lambda cols: pl.BlockSpec((tm, cols), lambda i: (i, 0))
    kv_spec = pl.BlockSpec((None, tm * HEADS, HEAD_DIM), lambda i: (layer, i, 0))
    kv_shape = jax.ShapeDtypeStruct((n_layers, n * HEADS, HEAD_DIM), F32)
    out_shape = (
        jax.ShapeDtypeStruct((n, width), BF16),
        kv_shape,
        kv_shape,
        jax.ShapeDtypeStruct((n, width), BF16),
        jax.ShapeDtypeStruct((n, width), BF16),
        jax.ShapeDtypeStruct((n, width), F32),
        jax.ShapeDtypeStruct((n, HEADS), F32),
    )
    in_specs = [row(d), _const_spec((1, d)), _const_spec(w.shape), _const_spec(wf.shape),
                _const_spec((1, HEADS)), _const_spec((1, HEAD_DIM)), _const_spec((1, HEAD_DIM))]
    args = [x, g.reshape(1, d), w, wf, b_f.reshape(1, HEADS), q_g.reshape(1, HEAD_DIM), k_g.reshape(1, HEAD_DIM)]
    aliases = {}
    if kv_prev is not None:
        in_specs += [pl.BlockSpec(memory_space=pl.ANY)] * 2
        aliases = {len(args): 1, len(args) + 1: 2}
        args += list(kv_prev)
    return pl.pallas_call(
        _fox_inproj_kernel,
        grid=(n // tm,),
        in_specs=in_specs,
        out_specs=(row(width), kv_spec, kv_spec, row(width), row(width), row(width), row(HEADS)),
        out_shape=out_shape,
        input_output_aliases=aliases,
        compiler_params=_params(1),
        name="fox_inproj",
    )(*args)


def _split3(x):
    hi = x.astype(BF16)
    r1 = x - hi.astype(F32)
    mid = r1.astype(BF16)
    lo = (r1 - mid.astype(F32)).astype(BF16)
    return hi, mid, lo


def _cumsum_kernel(x_ref, o_ref, *, chained):
    _, nb, lb = x_ref.shape
    r = lax.broadcasted_iota(jnp.int32, (lb, lb), 0)
    c = lax.broadcasted_iota(jnp.int32, (lb, lb), 1)
    upper = (r <= c).astype(BF16)
    if chained:
        rb = lax.broadcasted_iota(jnp.int32, (nb, nb), 0)
        cb = lax.broadcasted_iota(jnp.int32, (nb, nb), 1)
    for hd in range(HEADS):
        hi, mid, lo = _split3(x_ref[hd])
        cs = _dot(hi, upper) + _dot(mid, upper) + _dot(lo, upper)
        if chained:
            tot = jnp.broadcast_to(cs[:, lb - 1:lb], (nb, nb))
            tot_row = jnp.sum(jnp.where(rb == cb, tot, 0.0), axis=0, keepdims=True)
            offs = jnp.sum(jnp.where(cb < rb, jnp.broadcast_to(tot_row, (nb, nb)), 0.0),
                           axis=1, keepdims=True)
            cs = cs + offs
        o_ref[hd] = cs


def _cumsum_lanes(x, chained):
    bsz, _, nb, lb = x.shape
    spec = pl.BlockSpec((None, HEADS, nb, lb), lambda b: (b, 0, 0, 0))
    return pl.pallas_call(
        functools.partial(_cumsum_kernel, chained=chained),
        grid=(bsz,),
        in_specs=[spec],
        out_specs=spec,
        out_shape=jax.ShapeDtypeStruct(x.shape, F32),
        compiler_params=_params(1),
        name="logf_cumsum",
    )(x)


def _softmax_block(q, k, v, bias, carry, mask):
    m, l, acc = carry
    s = _dot_nt(q, k) * (HEAD_DIM ** -0.5) + bias
    if mask is not None:
        s = jnp.where(mask, s, NEG_INF)
    m_new = jnp.maximum(m, jnp.max(s, axis=1, keepdims=True))
    p = jnp.exp(s - m_new)
    alpha = jnp.exp(m - m_new)
    l = alpha * l + jnp.sum(p, axis=1, keepdims=True)
    acc = alpha * acc + _dot(p.astype(BF16), v)
    return m_new, l, acc


def _row_to_col(row, eye):
    n = row.shape[1]
    return jnp.sum(jnp.where(eye, jnp.broadcast_to(row, (n, n)), 0.0), axis=1, keepdims=True)


def _prompt_attn_kernel(q_ref, k_ref, v_ref, sg_ref, f_ref, og_ref):
    t = ATTN_BLOCK
    nblk = f_ref.shape[0]
    rows = lax.broadcasted_iota(jnp.int32, (t, t), 0)
    cols = lax.broadcasted_iota(jnp.int32, (t, t), 1)
    eye = rows == cols
    causal = cols <= rows
    for qi in range(nblk):
        qs = slice(qi * t, (qi + 1) * t)
        q = q_ref[qs, :]
        f_q = _row_to_col(f_ref[qi], eye)

        def body(ki, carry, q=q, f_q=f_q):
            start = pl.multiple_of(ki * t, t)
            return _softmax_block(q, k_ref[pl.ds(start, t), :], v_ref[pl.ds(start, t), :],
                                  f_q - f_ref[ki], carry, None)

        carry = (jnp.full((t, 1), NEG_INF, F32), jnp.zeros((t, 1), F32), jnp.zeros((t, HEAD_DIM), F32))
        carry = lax.fori_loop(0, qi, body, carry)
        _, l, acc = _softmax_block(q, k_ref[qs, :], v_ref[qs, :], f_q - f_ref[qi], carry, causal)
        og_ref[qs, :] = (acc / l * sg_ref[qs, :]).astype(BF16)


def _prompt_attn(q16, k16, v16, sg, fcum, bsz, s_len):
    t = ATTN_BLOCK
    n, width = q16.shape
    blk = pl.BlockSpec((s_len, HEAD_DIM), lambda b, h: (b, h))
    return pl.pallas_call(
        _prompt_attn_kernel,
        grid=(bsz, HEADS),
        in_specs=[blk, blk, blk, blk,
                  pl.BlockSpec((None, None, s_len // t, 1, t), lambda b, h: (b, h, 0, 0, 0))],
        out_specs=blk,
        out_shape=jax.ShapeDtypeStruct((n, width), BF16),
        compiler_params=_params(2),
        name="prompt_attn",
    )(q16, k16, v16, sg, fcum)


def _sample_attn_kernel(q_ref, ck_ref, cv_ref, kn_ref, vn_ref, sg_ref, fp_ref, fl_ref, fn_ref,
                        og_ref, m_ref, l_ref, acc_ref, fq_ref):
    ki = pl.program_id(1)
    last = pl.num_programs(1) - 1
    t = q_ref.shape[0]
    tk = fp_ref.shape[1]
    scale = HEAD_DIM ** -0.5
    heads = range(HEADS)
    hs = lambda hd: slice(hd * HEAD_DIM, (hd + 1) * HEAD_DIM)
    rows = lax.broadcasted_iota(jnp.int32, (t, t), 0)
    cols = lax.broadcasted_iota(jnp.int32, (t, t), 1)

    @pl.when(ki == 0)
    def _():
        m_ref[...] = jnp.full(m_ref.shape, NEG_INF, F32)
        l_ref[...] = jnp.zeros(l_ref.shape, F32)
        acc_ref[...] = jnp.zeros(acc_ref.shape, F32)
        f_new = fl_ref[...] + fn_ref[...]
        for hd in heads:
            fq_ref[hd] = _row_to_col(f_new[hd:hd + 1, :], rows == cols)

    def update(with_new):
        f_new = fl_ref[...] + fn_ref[...]
        s = []
        for hd in heads:
            q = q_ref[:, hs(hd)]
            k = ck_ref[pl.ds(hd, tk, stride=HEADS), :].astype(BF16)
            segs = [_dot_nt(q, k) * scale + (fq_ref[hd] - fp_ref[hd:hd + 1, :])]
            if with_new:
                sn = _dot_nt(q, kn_ref[:, hs(hd)]) * scale + (fq_ref[hd] - f_new[hd:hd + 1, :])
                segs.append(jnp.where(cols <= rows, sn, NEG_INF))
            s.append(segs)
        m_old = [m_ref[hd] for hd in heads]
        m_new = [functools.reduce(jnp.maximum, [m_old[hd]] + [jnp.max(x, axis=1, keepdims=True) for x in s[hd]])
                 for hd in heads]
        p = [[jnp.exp(x - m_new[hd]) for x in s[hd]] for hd in heads]
        alpha = [jnp.exp(m_old[hd] - m_new[hd]) for hd in heads]
        l_new = [alpha[hd] * l_ref[hd] + sum(jnp.sum(x, axis=1, keepdims=True) for x in p[hd]) for hd in heads]
        for hd in heads:
            v = cv_ref[pl.ds(hd, tk, stride=HEADS), :].astype(BF16)
            pv = _dot(p[hd][0].astype(BF16), v)
            if with_new:
                pv = pv + _dot(p[hd][1].astype(BF16), vn_ref[:, hs(hd)])
            acc = alpha[hd] * acc_ref[hd] + pv
            if with_new:
                og_ref[:, hs(hd)] = (acc / l_new[hd] * sg_ref[:, hs(hd)]).astype(BF16)
            else:
                m_ref[hd] = m_new[hd]
                l_ref[hd] = l_new[hd]
                acc_ref[hd] = acc

    @pl.when(ki != last)
    def _():
        update(False)

    @pl.when(ki == last)
    def _():
        update(True)


def _sample_attn(q16, cache_k, cache_v, layer, k16, v16, sg, f_past, f_last, f_new, bsz, t):
    n, width = q16.shape
    p_len = f_past.shape[2]
    tk = SAMPLE_KV_BLOCK
    new = pl.BlockSpec((t, width), lambda b, k: (b, 0))
    past = pl.BlockSpec((None, None, tk * HEADS, HEAD_DIM), lambda b, k: (layer, b, k, 0))
    return pl.pallas_call(
        _sample_attn_kernel,
        grid=(bsz, p_len // tk),
        in_specs=[new, past, past, new, new, new,
                  pl.BlockSpec((None, HEADS, tk), lambda b, k: (b, 0, k)),
                  pl.BlockSpec((None, HEADS, 1), lambda b, k: (b, 0, 0)),
                  pl.BlockSpec((None, HEADS, t), lambda b, k: (b, 0, 0))],
        out_specs=new,
        out_shape=jax.ShapeDtypeStruct((n, width), BF16),
        scratch_shapes=[pltpu.VMEM((HEADS, t, 1), F32), pltpu.VMEM((HEADS, t, 1), F32),
                        pltpu.VMEM((HEADS, t, HEAD_DIM), F32), pltpu.VMEM((HEADS, t, 1), F32)],
        compiler_params=_params(2),
        name="sample_attn",
    )(q16, cache_k, cache_v, k16, v16, sg, f_past, f_last, f_new)


def _outproj_kernel(x_ref, a_ref, w_ref, o_ref):
    o_ref[...] = x_ref[...] + _dot(a_ref[...], w_ref[...])


def _outproj(x, a, w):
    n, d = x.shape
    tm = OUTPROJ_ROW_TILE
    return pl.pallas_call(
        _outproj_kernel,
        grid=(n // tm,),
        in_specs=[pl.BlockSpec((tm, d), lambda i: (i, 0)),
                  pl.BlockSpec((tm, a.shape[1]), lambda i: (i, 0)),
                  _const_spec(w.shape)],
        out_specs=pl.BlockSpec((tm, d), lambda i: (i, 0)),
        out_shape=jax.ShapeDtypeStruct((n, d), F32),
        compiler_params=_params(1),
        name="outproj",
    )(x, a, w)


def _gmlp_kernel(x_ref, g_ref, win_ref, vg_ref, mix_ref, bias_ref, wout_ref, *rest, emit_v):
    if emit_v:
        o_ref, vn_ref, a_ref = rest
    else:
        o_ref, a_ref = rest
    tm = x_ref.shape[0]
    bw = wout_ref.shape[0]
    gd = bw // GROUPS
    x = x_ref[...]
    h = _rmsnorm(x, g_ref[...]).astype(BF16)
    v = _dot(h, win_ref[:, bw:bw + gd])
    for gi in range(GROUPS):
        sl = slice(gi * gd, (gi + 1) * gd)
        u = _dot(h, win_ref[:, sl])
        gate = _dot(h, win_ref[:, 2 * bw + gi * gd:2 * bw + (gi + 1) * gd])
        vn = _rmsnorm(v, vg_ref[:, sl])
        if gi + 1 < GROUPS:
            v = _dot(h, win_ref[:, bw + (gi + 1) * gd:bw + (gi + 2) * gd])
        if emit_v:
            vn_ref[:, sl] = vn
        vn16 = vn.astype(BF16)
        mixed = jnp.concatenate(
            [_dot(mix_ref[gi], vn16[ci * MLP_CHUNK:(ci + 1) * MLP_CHUNK, :]) + bias_ref[:, gi:gi + 1]
             for ci in range(tm // MLP_CHUNK)], axis=0)
        a_ref[:, sl] = (u * mixed * jax.nn.silu(gate)).astype(BF16)
    o_ref[...] = x + _dot(a_ref[...], wout_ref[...])


def _gmlp_layer(x, g, win, vg, mix, bias, wout, emit_v):
    n, d = x.shape
    bw = wout.shape[0]
    tm = GMLP_ROW_TILE
    row = lambda cols: pl.BlockSpec((tm, cols), lambda i: (i, 0))
    out_shape = [jax.ShapeDtypeStruct((n, d), F32)]
    out_specs = [row(d)]
    if emit_v:
        out_shape.append(jax.ShapeDtypeStruct((n, bw), F32))
        out_specs.append(row(bw))
    res = pl.pallas_call(
        functools.partial(_gmlp_kernel, emit_v=emit_v),
        grid=(n // tm,),
        in_specs=[row(d), _const_spec((1, d)), _const_spec(win.shape), _const_spec((1, bw)),
                  _const_spec(mix.shape), _const_spec(bias.shape), _const_spec(wout.shape)],
        out_specs=tuple(out_specs),
        out_shape=tuple(out_shape),
        scratch_shapes=[pltpu.VMEM((tm, bw), BF16)],
        compiler_params=_params(1),
        name="gmlp_layer",
    )(x, g.reshape(1, d), win, vg.reshape(1, bw), mix, bias, wout)
    return res if emit_v else res[0]


def _sgu_mask():
    c = jnp.arange(MLP_CHUNK) // STREAM_CHUNK
    return c[None, :] <= c[:, None]


def kernel(x_prompt, x_sample, cache_k, cache_v, cache_logf, norm_g, w_in_a, b_f, q_g, k_g, w_out_a,
           w_in_b, v_g, ws, bs, w_out_b):
    bsz, s_len, d = x_prompt.shape
    dbsz, t_new, _ = x_sample.shape
    p_len = cache_k.shape[2]
    width = HEADS * HEAD_DIM
    depth = norm_g.shape[0]
    assert s_len % ATTN_BLOCK == 0 and s_len % CUMSUM_BLOCK == 0 and p_len % CUMSUM_BLOCK == 0
    assert p_len % SAMPLE_KV_BLOCK == 0 and MLP_CHUNK == 2 * t_new and GMLP_ROW_TILE % MLP_CHUNK == 0

    xp = x_prompt.reshape(bsz * s_len, d)
    xs = x_sample.reshape(dbsz * t_new, d)
    n_a = w_in_a.shape[0]
    ck = cache_k.reshape(n_a, dbsz, p_len * HEADS, HEAD_DIM)
    cv = cache_v.reshape(n_a, dbsz, p_len * HEADS, HEAD_DIM)
    kv_p, kv_s = None, None
    lp_l, ls_l, sgu_l = [], [], []
    for i in range(depth):
        j = i // 2
        if i % 2 == 0:
            w = w_in_a[j][:, :4 * width].astype(BF16)
            wf = jnp.pad(w_in_a[j][:, 4 * width:], ((0, 0), (0, HEAD_DIM - HEADS))).astype(BF16)
            wo = w_out_a[j].astype(BF16)

            q16, k_all, v_all, k16, v16, sg, logf = _fox_inproj(xp, norm_g[i], w, wf, b_f[j], q_g[j], k_g[j],
                                                                j, n_a, kv_p)
            kv_p = (k_all, v_all)
            nb = s_len // CUMSUM_BLOCK
            lt = jnp.transpose(logf.reshape(bsz, s_len, HEADS), (0, 2, 1))
            fcum = _cumsum_lanes(lt.reshape(bsz, HEADS, nb, CUMSUM_BLOCK), chained=True)
            fcum = fcum.reshape(bsz, HEADS, s_len // ATTN_BLOCK, 1, ATTN_BLOCK)
            og = _prompt_attn(q16, k16, v16, sg, fcum, bsz, s_len)
            xp = _outproj(xp, og, wo)
            lp_l.append(logf.reshape(bsz, s_len, HEADS))

            q16, k_all, v_all, k16, v16, sg, logf = _fox_inproj(xs, norm_g[i], w, wf, b_f[j], q_g[j], k_g[j],
                                                                j, n_a, kv_s)
            kv_s = (k_all, v_all)
            nbp = p_len // CUMSUM_BLOCK
            cl = jnp.transpose(cache_logf[j], (0, 2, 1)).reshape(dbsz, HEADS, nbp, CUMSUM_BLOCK)
            f_past = _cumsum_lanes(cl, chained=True).reshape(dbsz, HEADS, p_len)
            lt = jnp.transpose(logf.reshape(dbsz, t_new, HEADS), (0, 2, 1))
            f_new = _cumsum_lanes(lt.reshape(1, HEADS, dbsz, t_new), chained=False)
            f_new = f_new.reshape(dbsz, HEADS, t_new)
            og = _sample_attn(q16, ck, cv, j, k16, v16, sg, f_past, f_past[:, :, p_len - 1:], f_new, dbsz, t_new)
            xs = _outproj(xs, og, wo)
            ls_l.append(logf.reshape(dbsz, t_new, HEADS))
        else:
            win = w_in_b[j].astype(BF16)
            wout = w_out_b[j].astype(BF16)
            bw = wout.shape[0]
            mix_p = (ws[j] * _sgu_mask()[None]).astype(BF16)
            bias_p = jnp.transpose(bs[j])
            xp = _gmlp_layer(xp, norm_g[i], win, v_g[j], mix_p, bias_p, wout, emit_v=False)
            a = (ws[j] * _sgu_mask()[None])[:, :t_new, :t_new]
            z = jnp.zeros_like(a)
            mix_s = jnp.concatenate([jnp.concatenate([a, z], axis=2),
                                     jnp.concatenate([z, a], axis=2)], axis=1).astype(BF16)
            bias_s = jnp.transpose(jnp.concatenate([bs[j][:, :t_new], bs[j][:, :t_new]], axis=1))
            xs, sv = _gmlp_layer(xs, norm_g[i], win, v_g[j], mix_s, bias_s, wout, emit_v=True)
            sgu_l.append(sv.reshape(dbsz, t_new, bw))
    kv5 = lambda a, nb_, t_: a.reshape(n_a, nb_, t_, HEADS, HEAD_DIM)
    return (xp.reshape(bsz, s_len, d), xs.reshape(dbsz, t_new, d),
            kv5(kv_p[0], bsz, s_len), kv5(kv_p[1], bsz, s_len), jnp.stack(lp_l),
            kv5(kv_s[0], dbsz, t_new), kv5(kv_s[1], dbsz, t_new), jnp.stack(ls_l), jnp.stack(sgu_l))
```

```python
import functools

import jax
import jax.numpy as jnp
from jax import lax
from jax.experimental import pallas as pl
from jax.experimental.pallas import tpu as pltpu

F32 = jnp.float32
BF16 = jnp.bfloat16

EPS = 1e-6
NEG_INF = -1e30
LOG2E = 1.4426950408889634

HEADS = 8
HEAD_DIM = 128
Q_PRESCALE = HEAD_DIM ** -0.5 * LOG2E
GROUPS = 8
MLP_CHUNK = 128
STREAM_CHUNK = 64

V7X_VMEM_BYTES = 64 * 1024 * 1024
VMEM_LIMIT_BYTES = V7X_VMEM_BYTES - 8 * 1024 * 1024

ROW_TILE = 512
GMLP_ROW_TILE = 512
OUTPROJ_ROW_TILE = 1024
ATTN_BLOCK = 512
HEADS_PER_STEP = 4
SAMPLE_KV_BLOCK = 1024
CUMSUM_BLOCK = 256


def _params(n_axes):
    return pltpu.CompilerParams(dimension_semantics=("arbitrary",) * n_axes,
                                vmem_limit_bytes=VMEM_LIMIT_BYTES)


def _const_spec(shape):
    nd = len(shape)
    return pl.BlockSpec(shape, lambda *_: (0,) * nd, pipeline_mode=pl.Buffered(1))


def _rmsnorm(x, g):
    ms = jnp.mean(x * x, axis=-1, keepdims=True)
    return x * lax.rsqrt(ms + EPS) * g


def _dot(a, b):
    return jnp.dot(a, b, preferred_element_type=F32)


def _dot_nt(a, b):
    return lax.dot_general(a, b, (((1,), (1,)), ((), ())), preferred_element_type=F32)


def _fox_inproj_kernel(x_ref, g_ref, w_ref, wf_ref, bf_ref, qg_ref, kg_ref, *rest, layer, first, v_transposed):
    q16_ref, k32_ref, v32_ref, k16_ref, v16_ref, sg_ref, logf_ref = rest[-7:]
    if first:
        for slot in range(k32_ref.shape[0]):
            if slot != layer:
                k32_ref[slot] = jnp.zeros(k32_ref.shape[1:], F32)
                v32_ref[slot] = jnp.zeros(v32_ref.shape[1:], F32)
        k32_ref = k32_ref.at[layer]
        v32_ref = v32_ref.at[layer]
    tm = x_ref.shape[0]
    width = HEADS * HEAD_DIM
    h = _rmsnorm(x_ref[...], g_ref[...]).astype(BF16)

    zq = _dot(h, w_ref[:, 0:width])
    for hd in range(HEADS):
        sl = slice(hd * HEAD_DIM, (hd + 1) * HEAD_DIM)
        q16_ref[:, sl] = (_rmsnorm(zq[:, sl], qg_ref[...]) * Q_PRESCALE).astype(BF16)

    zk = _dot(h, w_ref[:, width:2 * width])
    for hd in range(HEADS):
        sl = slice(hd * HEAD_DIM, (hd + 1) * HEAD_DIM)
        kn = _rmsnorm(zk[:, sl], kg_ref[...])
        k32_ref[pl.ds(hd, tm, stride=HEADS), :] = kn
        k16_ref[:, sl] = kn.astype(BF16)

    zv = _dot(h, w_ref[:, 2 * width:3 * width])
    for hd in range(HEADS):
        sl = slice(hd * HEAD_DIM, (hd + 1) * HEAD_DIM)
        v32_ref[pl.ds(hd, tm, stride=HEADS), :] = zv[:, sl]
        if v_transposed:
            v16_ref[sl, :] = jnp.transpose(zv[:, sl]).astype(BF16)
    if not v_transposed:
        v16_ref[...] = zv.astype(BF16)

    zg = _dot(h, w_ref[:, 3 * width:4 * width])
    sg_ref[...] = jax.nn.silu(zg)

    zf = _dot(h, wf_ref[...])
    logf_ref[...] = jax.nn.log_sigmoid(zf[:, 0:HEADS] + bf_ref[...])


def _fox_inproj(x, g, w, wf, b_f, q_g, k_g, layer, n_layers, kv_prev, v_transposed):
    n, d = x.shape
    width = HEADS * HEAD_DIM
    tm = ROW_TILE
    first = kv_prev is None
    row = lambda cols: pl.BlockSpec((tm, cols), lambda i: (i, 0))
    if first:
        kv_spec = pl.BlockSpec((n_layers, tm * HEADS, HEAD_DIM), lambda i: (0, i, 0))
    else:
        kv_spec = pl.BlockSpec((None, tm * HEADS, HEAD_DIM), lambda i: (layer, i, 0))
    kv_shape = jax.ShapeDtypeStruct((n_layers, n * HEADS, HEAD_DIM), F32)
    if v_transposed:
        v16_shape = jax.ShapeDtypeStruct((n // tm, width, tm), BF16)
        v16_spec = pl.BlockSpec((None, width, tm), lambda i: (i, 0, 0))
    else:
        v16_shape = jax.ShapeDtypeStruct((n, width), BF16)
        v16_spec = row(width)
    out_shape = (
        jax.ShapeDtypeStruct((n, width), BF16),
        kv_shape,
        kv_shape,
        jax.ShapeDtypeStruct((n, width), BF16),
        v16_shape,
        jax.ShapeDtypeStruct((n, width), F32),
        jax.ShapeDtypeStruct((n, HEADS), F32),
    )
    in_specs = [row(d), _const_spec((1, d)), _const_spec(w.shape), _const_spec(wf.shape),
                _const_spec((1, HEADS)), _const_spec((1, HEAD_DIM)), _const_spec((1, HEAD_DIM))]
    args = [x, g.reshape(1, d), w, wf, b_f.reshape(1, HEADS), q_g.reshape(1, HEAD_DIM), k_g.reshape(1, HEAD_DIM)]
    aliases = {}
    if not first:
        in_specs += [pl.BlockSpec(memory_space=pl.ANY)] * 2
        aliases = {len(args): 1, len(args) + 1: 2}
        args += list(kv_prev)
    return pl.pallas_call(
        functools.partial(_fox_inproj_kernel, layer=layer, first=first, v_transposed=v_transposed),
        grid=(n // tm,),
        in_specs=in_specs,
        out_specs=(row(width), kv_spec, kv_spec, row(width), v16_spec, row(width), row(HEADS)),
        out_shape=out_shape,
        input_output_aliases=aliases,
        compiler_params=_params(1),
        name="fox_inproj",
    )(*args)


def _split3(x):
    hi = x.astype(BF16)
    r1 = x - hi.astype(F32)
    mid = r1.astype(BF16)
    lo = (r1 - mid.astype(F32)).astype(BF16)
    return hi, mid, lo


def _cumsum_kernel(x_ref, o_ref, *, chained):
    _, nb, lb = x_ref.shape
    r = lax.broadcasted_iota(jnp.int32, (lb, lb), 0)
    c = lax.broadcasted_iota(jnp.int32, (lb, lb), 1)
    upper = (r <= c).astype(BF16)
    if chained:
        rb = lax.broadcasted_iota(jnp.int32, (nb, nb), 0)
        cb = lax.broadcasted_iota(jnp.int32, (nb, nb), 1)
    for hd in range(HEADS):
        hi, mid, lo = _split3(x_ref[hd])
        cs = _dot(hi, upper) + _dot(mid, upper) + _dot(lo, upper)
        if chained:
            tot = jnp.broadcast_to(cs[:, lb - 1:lb], (nb, nb))
            tot_row = jnp.sum(jnp.where(rb == cb, tot, 0.0), axis=0, keepdims=True)
            offs = jnp.sum(jnp.where(cb < rb, jnp.broadcast_to(tot_row, (nb, nb)), 0.0),
                           axis=1, keepdims=True)
            cs = cs + offs
        o_ref[hd] = cs


def _cumsum_lanes(x, chained):
    bsz, _, nb, lb = x.shape
    spec = pl.BlockSpec((None, HEADS, nb, lb), lambda b: (b, 0, 0, 0))
    return pl.pallas_call(
        functools.partial(_cumsum_kernel, chained=chained),
        grid=(bsz,),
        in_specs=[spec],
        out_specs=spec,
        out_shape=jax.ShapeDtypeStruct(x.shape, F32),
        compiler_params=_params(1),
        name="logf_cumsum",
    )(x)


def _row_to_col(row, eye):
    n = row.shape[1]
    return jnp.sum(jnp.where(eye, jnp.broadcast_to(row, (n, n)), 0.0), axis=1, keepdims=True)


def _prompt_attn_kernel(q_ref, k_ref, vt_ref, sg_ref, f_ref, og_ref, m_s, l_s, acc_s, fk_s):
    t = ATTN_BLOCK
    hp, nblk = f_ref.shape[0], f_ref.shape[1]
    heads = range(hp)
    hs = lambda h: slice(h * HEAD_DIM, (h + 1) * HEAD_DIM)
    key_i = lax.broadcasted_iota(jnp.int32, (t, t), 0)
    qry_i = lax.broadcasted_iota(jnp.int32, (t, t), 1)
    causal = key_i <= qry_i
    for h in heads:
        for blk in range(nblk):
            fk_s[h, blk] = _row_to_col(f_ref[h, blk] * LOG2E, key_i == qry_i)

    def step(q, fq, k_of, vt_of, fk_of, masked):
        st = [_dot_nt(k_of(h), q[h]) + (fq[h] - fk_of(h)) for h in heads]
        if masked:
            st = [jnp.where(causal, x, NEG_INF) for x in st]
        m_old = [m_s[h] for h in heads]
        m_new = [jnp.maximum(m_old[h], jnp.max(st[h], axis=0, keepdims=True)) for h in heads]
        pt = [jnp.exp2(st[h] - m_new[h]) for h in heads]
        alpha = [jnp.exp2(m_old[h] - m_new[h]) for h in heads]
        for h in heads:
            m_s[h] = m_new[h]
            l_s[h] = alpha[h] * l_s[h] + jnp.sum(pt[h], axis=0, keepdims=True)
        for h in heads:
            acc_s[h] = alpha[h] * acc_s[h] + _dot(vt_of(h), pt[h].astype(BF16))

    for qi in range(nblk):
        qs = slice(qi * t, (qi + 1) * t)
        q = [q_ref[qs, hs(h)] for h in heads]
        fq = [f_ref[h, qi] * LOG2E for h in heads]
        m_s[...] = jnp.full(m_s.shape, NEG_INF, F32)
        l_s[...] = jnp.zeros(l_s.shape, F32)
        acc_s[...] = jnp.zeros(acc_s.shape, F32)

        def body(ki, carry, q=q, fq=fq):
            start = pl.multiple_of(ki * t, t)
            step(q, fq, lambda h: k_ref[pl.ds(start, t), hs(h)], lambda h: vt_ref[ki, hs(h), :],
                 lambda h: fk_s[h, ki], False)
            return carry

        lax.fori_loop(0, qi, body, 0)
        step(q, fq, lambda h: k_ref[qs, hs(h)], lambda h: vt_ref[qi, hs(h), :], lambda h: fk_s[h, qi], True)
        for h in heads:
            o = jnp.transpose(acc_s[h] / l_s[h])
            og_ref[qs, hs(h)] = (o * sg_ref[qs, hs(h)]).astype(BF16)


def _prompt_attn(q16, k16, vt16, sg, fcum, bsz, s_len):
    t = ATTN_BLOCK
    hp = HEADS_PER_STEP
    nblk = s_len // t
    n, width = q16.shape
    blk = pl.BlockSpec((s_len, hp * HEAD_DIM), lambda b, g: (b, g))
    return pl.pallas_call(
        _prompt_attn_kernel,
        grid=(bsz, HEADS // hp),
        in_specs=[blk, blk,
                  pl.BlockSpec((nblk, hp * HEAD_DIM, t), lambda b, g: (b, g, 0)),
                  blk,
                  pl.BlockSpec((None, hp, nblk, 1, t), lambda b, g: (b, g, 0, 0, 0))],
        out_specs=blk,
        out_shape=jax.ShapeDtypeStruct((n, width), BF16),
        scratch_shapes=[pltpu.VMEM((hp, 1, t), F32), pltpu.VMEM((hp, 1, t), F32),
                        pltpu.VMEM((hp, HEAD_DIM, t), F32), pltpu.VMEM((hp, nblk, t, 1), F32)],
        compiler_params=_params(2),
        name="prompt_attn",
    )(q16, k16, vt16, sg, fcum)


def _sample_attn_kernel(q_ref, ck_ref, cv_ref, kn_ref, vn_ref, sg_ref, fp_ref, fl_ref, fn_ref,
                        og_ref, m_ref, l_ref, acc_ref, fq_ref):
    ki = pl.program_id(1)
    last = pl.num_programs(1) - 1
    t = q_ref.shape[0]
    tk = fp_ref.shape[1]
    heads = range(HEADS)
    hs = lambda hd: slice(hd * HEAD_DIM, (hd + 1) * HEAD_DIM)
    rows = lax.broadcasted_iota(jnp.int32, (t, t), 0)
    cols = lax.broadcasted_iota(jnp.int32, (t, t), 1)

    @pl.when(ki == 0)
    def _():
        m_ref[...] = jnp.full(m_ref.shape, NEG_INF, F32)
        l_ref[...] = jnp.zeros(l_ref.shape, F32)
        acc_ref[...] = jnp.zeros(acc_ref.shape, F32)
        f_new = (fl_ref[...] + fn_ref[...]) * LOG2E
        for hd in heads:
            fq_ref[hd] = _row_to_col(f_new[hd:hd + 1, :], rows == cols)

    def update(with_new):
        f_new = (fl_ref[...] + fn_ref[...]) * LOG2E
        f_past = fp_ref[...] * LOG2E
        s = []
        for hd in heads:
            q = q_ref[:, hs(hd)]
            k = ck_ref[pl.ds(hd, tk, stride=HEADS), :].astype(BF16)
            segs = [_dot_nt(q, k) + (fq_ref[hd] - f_past[hd:hd + 1, :])]
            if with_new:
                sn = _dot_nt(q, kn_ref[:, hs(hd)]) + (fq_ref[hd] - f_new[hd:hd + 1, :])
                segs.append(jnp.where(cols <= rows, sn, NEG_INF))
            s.append(segs)
        m_old = [m_ref[hd] for hd in heads]
        m_new = [functools.reduce(jnp.maximum, [m_old[hd]] + [jnp.max(x, axis=1, keepdims=True) for x in s[hd]])
                 for hd in heads]
        p = [[jnp.exp2(x - m_new[hd]) for x in s[hd]] for hd in heads]
        alpha = [jnp.exp2(m_old[hd] - m_new[hd]) for hd in heads]
        l_new = [alpha[hd] * l_ref[hd] + sum(jnp.sum(x, axis=1, keepdims=True) for x in p[hd]) for hd in heads]
        for hd in heads:
            v = cv_ref[pl.ds(hd, tk, stride=HEADS), :].astype(BF16)
            pv = _dot(p[hd][0].astype(BF16), v)
            if with_new:
                pv = pv + _dot(p[hd][1].astype(BF16), vn_ref[:, hs(hd)])
            acc = alpha[hd] * acc_ref[hd] + pv
            if with_new:
                og_ref[:, hs(hd)] = (acc / l_new[hd] * sg_ref[:, hs(hd)]).astype(BF16)
            else:
                m_ref[hd] = m_new[hd]
                l_ref[hd] = l_new[hd]
                acc_ref[hd] = acc

    @pl.when(ki != last)
    def _():
        update(False)

    @pl.when(ki == last)
    def _():
        update(True)


def _sample_attn(q16, cache_k, cache_v, layer, k16, v16, sg, f_past, f_last, f_new, bsz, t):
    n, width = q16.shape
    p_len = f_past.shape[2]
    tk = SAMPLE_KV_BLOCK
    new = pl.BlockSpec((t, width), lambda b, k: (b, 0))
    past = pl.BlockSpec((None, None, tk * HEADS, HEAD_DIM), lambda b, k: (layer, b, k, 0))
    return pl.pallas_call(
        _sample_attn_kernel,
        grid=(bsz, p_len // tk),
        in_specs=[new, past, past, new, new, new,
                  pl.BlockSpec((None, HEADS, tk), lambda b, k: (b, 0, k)),
                  pl.BlockSpec((None, HEADS, 1), lambda b, k: (b, 0, 0)),
                  pl.BlockSpec((None, HEADS, t), lambda b, k: (b, 0, 0))],
        out_specs=new,
        out_shape=jax.ShapeDtypeStruct((n, width), BF16),
        scratch_shapes=[pltpu.VMEM((HEADS, t, 1), F32), pltpu.VMEM((HEADS, t, 1), F32),
                        pltpu.VMEM((HEADS, t, HEAD_DIM), F32), pltpu.VMEM((HEADS, t, 1), F32)],
        compiler_params=_params(2),
        name="sample_attn",
    )(q16, cache_k, cache_v, k16, v16, sg, f_past, f_last, f_new)


def _outproj_kernel(x_ref, a_ref, w_ref, o_ref):
    o_ref[...] = x_ref[...] + _dot(a_ref[...], w_ref[...])


def _outproj(x, a, w):
    n, d = x.shape
    tm = OUTPROJ_ROW_TILE
    return pl.pallas_call(
        _outproj_kernel,
        grid=(n // tm,),
        in_specs=[pl.BlockSpec((tm, d), lambda i: (i, 0)),
                  pl.BlockSpec((tm, a.shape[1]), lambda i: (i, 0)),
                  _const_spec(w.shape)],
        out_specs=pl.BlockSpec((tm, d), lambda i: (i, 0)),
        out_shape=jax.ShapeDtypeStruct((n, d), F32),
        compiler_params=_params(1),
        name="outproj",
    )(x, a, w)


def _gmlp_kernel(x_ref, g_ref, win_ref, vg_ref, mix_ref, bias_ref, wout_ref, *rest, emit_v):
    if emit_v:
        o_ref, vn_ref, a_ref = rest
    else:
        o_ref, a_ref = rest
    tm = x_ref.shape[0]
    bw = wout_ref.shape[0]
    gd = bw // GROUPS
    x = x_ref[...]
    h = _rmsnorm(x, g_ref[...]).astype(BF16)
    v = _dot(h, win_ref[:, bw:bw + gd])
    for gi in range(GROUPS):
        sl = slice(gi * gd, (gi + 1) * gd)
        u = _dot(h, win_ref[:, sl])
        gate = _dot(h, win_ref[:, 2 * bw + gi * gd:2 * bw + (gi + 1) * gd])
        vn = _rmsnorm(v, vg_ref[:, sl])
        if gi + 1 < GROUPS:
            v = _dot(h, win_ref[:, bw + (gi + 1) * gd:bw + (gi + 2) * gd])
        if emit_v:
            vn_ref[:, sl] = vn
        vn16 = vn.astype(BF16)
        mixed = jnp.concatenate(
            [_dot(mix_ref[gi], vn16[ci * MLP_CHUNK:(ci + 1) * MLP_CHUNK, :]) + bias_ref[:, gi:gi + 1]
             for ci in range(tm // MLP_CHUNK)], axis=0)
        a_ref[:, sl] = (u * mixed * jax.nn.silu(gate)).astype(BF16)
    o_ref[...] = x + _dot(a_ref[...], wout_ref[...])


def _gmlp_layer(x, g, win, vg, mix, bias, wout, emit_v):
    n, d = x.shape
    bw = wout.shape[0]
    tm = GMLP_ROW_TILE
    row = lambda cols: pl.BlockSpec((tm, cols), lambda i: (i, 0))
    out_shape = [jax.ShapeDtypeStruct((n, d), F32)]
    out_specs = [row(d)]
    if emit_v:
        out_shape.append(jax.ShapeDtypeStruct((n, bw), F32))
        out_specs.append(row(bw))
    res = pl.pallas_call(
        functools.partial(_gmlp_kernel, emit_v=emit_v),
        grid=(n // tm,),
        in_specs=[row(d), _const_spec((1, d)), _const_spec(win.shape), _const_spec((1, bw)),
                  _const_spec(mix.shape), _const_spec(bias.shape), _const_spec(wout.shape)],
        out_specs=tuple(out_specs),
        out_shape=tuple(out_shape),
        scratch_shapes=[pltpu.VMEM((tm, bw), BF16)],
        compiler_params=_params(1),
        name="gmlp_layer",
    )(x, g.reshape(1, d), win, vg.reshape(1, bw), mix, bias, wout)
    return res if emit_v else res[0]


def _sgu_mask():
    c = jnp.arange(MLP_CHUNK) // STREAM_CHUNK
    return c[None, :] <= c[:, None]


def kernel(x_prompt, x_sample, cache_k, cache_v, cache_logf, norm_g, w_in_a, b_f, q_g, k_g, w_out_a,
           w_in_b, v_g, ws, bs, w_out_b):
    bsz, s_len, d = x_prompt.shape
    dbsz, t_new, _ = x_sample.shape
    p_len = cache_k.shape[2]
    width = HEADS * HEAD_DIM
    depth = norm_g.shape[0]
    assert ROW_TILE == ATTN_BLOCK and HEADS % HEADS_PER_STEP == 0
    assert s_len % ATTN_BLOCK == 0 and s_len % CUMSUM_BLOCK == 0 and p_len % CUMSUM_BLOCK == 0
    assert p_len % SAMPLE_KV_BLOCK == 0 and MLP_CHUNK == 2 * t_new and GMLP_ROW_TILE % MLP_CHUNK == 0

    xp = x_prompt.reshape(bsz * s_len, d)
    xs = x_sample.reshape(dbsz * t_new, d)
    n_a = w_in_a.shape[0]
    ck = cache_k.reshape(n_a, dbsz, p_len * HEADS, HEAD_DIM)
    cv = cache_v.reshape(n_a, dbsz, p_len * HEADS, HEAD_DIM)
    kv_p, kv_s = None, None
    lp_l, ls_l, sgu_l = [], [], []
    for i in range(depth):
        j = i // 2
        if i % 2 == 0:
            w = w_in_a[j][:, :4 * width].astype(BF16)
            wf = jnp.pad(w_in_a[j][:, 4 * width:], ((0, 0), (0, HEAD_DIM - HEADS))).astype(BF16)
            wo = w_out_a[j].astype(BF16)

            q16, k_all, v_all, k16, v16, sg, logf = _fox_inproj(xp, norm_g[i], w, wf, b_f[j], q_g[j], k_g[j],
                                                                j, n_a, kv_p, v_transposed=True)
            kv_p = (k_all, v_all)
            nb = s_len // CUMSUM_BLOCK
            lt = jnp.transpose(logf.reshape(bsz, s_len, HEADS), (0, 2, 1))
            fcum = _cumsum_lanes(lt.reshape(bsz, HEADS, nb, CUMSUM_BLOCK), chained=True)
            fcum = fcum.reshape(bsz, HEADS, s_len // ATTN_BLOCK, 1, ATTN_BLOCK)
            og = _prompt_attn(q16, k16, v16, sg, fcum, bsz, s_len)
            xp = _outproj(xp, og, wo)
            lp_l.append(logf.reshape(bsz, s_len, HEADS))

            q16, k_all, v_all, k16, v16, sg, logf = _fox_inproj(xs, norm_g[i], w, wf, b_f[j], q_g[j], k_g[j],
                                                                j, n_a, kv_s, v_transposed=False)
            kv_s = (k_all, v_all)
            nbp = p_len // CUMSUM_BLOCK
            cl = jnp.transpose(cache_logf[j], (0, 2, 1)).reshape(dbsz, HEADS, nbp, CUMSUM_BLOCK)
            f_past = _cumsum_lanes(cl, chained=True).reshape(dbsz, HEADS, p_len)
            lt = jnp.transpose(logf.reshape(dbsz, t_new, HEADS), (0, 2, 1))
            f_new = _cumsum_lanes(lt.reshape(1, HEADS, dbsz, t_new), chained=False)
            f_new = f_new.reshape(dbsz, HEADS, t_new)
            og = _sample_attn(q16, ck, cv, j, k16, v16, sg, f_past, f_past[:, :, p_len - 1:], f_new, dbsz, t_new)
            xs = _outproj(xs, og, wo)
            ls_l.append(logf.reshape(dbsz, t_new, HEADS))
        else:
            win = w_in_b[j].astype(BF16)
            wout = w_out_b[j].astype(BF16)
            bw = wout.shape[0]
            mix_p = (ws[j] * _sgu_mask()[None]).astype(BF16)
            bias_p = jnp.transpose(bs[j])
            xp = _gmlp_layer(xp, norm_g[i], win, v_g[j], mix_p, bias_p, wout, emit_v=False)
            a = (ws[j] * _sgu_mask()[None])[:, :t_new, :t_new]
            z = jnp.zeros_like(a)
            mix_s = jnp.concatenate([jnp.concatenate([a, z], axis=2),
                                     jnp.concatenate([z, a], axis=2)], axis=1).astype(BF16)
            bias_s = jnp.transpose(jnp.concatenate([bs[j][:, :t_new], bs[j][:, :t_new]], axis=1))
            xs, sv = _gmlp_layer(xs, norm_g[i], win, v_g[j], mix_s, bias_s, wout, emit_v=True)
            sgu_l.append(sv.reshape(dbsz, t_new, bw))
    kv5 = lambda a, nb_, t_: a.reshape(n_a, nb_, t_, HEADS, HEAD_DIM)
    return (xp.reshape(bsz, s_len, d), xs.reshape(dbsz, t_new, d),
            kv5(kv_p[0], bsz, s_len), kv5(kv_p[1], bsz, s_len), jnp.stack(lp_l),
            kv5(kv_s[0], dbsz, t_new), kv5(kv_s[1], dbsz, t_new), jnp.stack(ls_l), jnp.stack(sgu_l))
```

```python
import functools

import jax
import jax.numpy as jnp
from jax import lax
from jax.experimental import pallas as pl
from jax.experimental.pallas import tpu as pltpu

F32 = jnp.float32
BF16 = jnp.bfloat16

EPS = 1e-6
NEG_INF = -1e30
LOG2E = 1.4426950408889634

HEADS = 8
HEAD_DIM = 128
Q_PRESCALE = HEAD_DIM ** -0.5 * LOG2E
GROUPS = 8
MLP_CHUNK = 128
STREAM_CHUNK = 64

V7X_VMEM_BYTES = 64 * 1024 * 1024
VMEM_LIMIT_BYTES = V7X_VMEM_BYTES - 8 * 1024 * 1024

ROW_TILE = 512
GMLP_ROW_TILE = 512
OUTPROJ_ROW_TILE = 1024
ATTN_BLOCK = 512
HEADS_PER_STEP = 4
SAMPLE_KV_BLOCK = 2048
CUMSUM_BLOCK = 256


def _params(n_axes):
    return pltpu.CompilerParams(dimension_semantics=("arbitrary",) * n_axes,
                                vmem_limit_bytes=VMEM_LIMIT_BYTES)


def _const_spec(shape):
    nd = len(shape)
    return pl.BlockSpec(shape, lambda *_: (0,) * nd, pipeline_mode=pl.Buffered(1))


def _rmsnorm(x, g):
    ms = jnp.mean(x * x, axis=-1, keepdims=True)
    return x * lax.rsqrt(ms + EPS) * g


def _dot(a, b):
    return jnp.dot(a, b, preferred_element_type=F32)


def _dot_nt(a, b):
    return lax.dot_general(a, b, (((1,), (1,)), ((), ())), preferred_element_type=F32)


def _fox_inproj_kernel(x_ref, g_ref, w_ref, wf_ref, bf_ref, qg_ref, kg_ref, *rest, layer, first, v_transposed):
    q16_ref, k32_ref, v32_ref, k16_ref, v16_ref, sg_ref, logf_ref = rest[-7:]
    if first:
        for slot in range(k32_ref.shape[0]):
            if slot != layer:
                k32_ref[slot] = jnp.zeros(k32_ref.shape[1:], F32)
                v32_ref[slot] = jnp.zeros(v32_ref.shape[1:], F32)
        k32_ref = k32_ref.at[layer]
        v32_ref = v32_ref.at[layer]
    tm = x_ref.shape[0]
    width = HEADS * HEAD_DIM
    h = _rmsnorm(x_ref[...], g_ref[...]).astype(BF16)

    zq = _dot(h, w_ref[:, 0:width])
    for hd in range(HEADS):
        sl = slice(hd * HEAD_DIM, (hd + 1) * HEAD_DIM)
        q16_ref[:, sl] = (_rmsnorm(zq[:, sl], qg_ref[...]) * Q_PRESCALE).astype(BF16)

    zk = _dot(h, w_ref[:, width:2 * width])
    for hd in range(HEADS):
        sl = slice(hd * HEAD_DIM, (hd + 1) * HEAD_DIM)
        kn = _rmsnorm(zk[:, sl], kg_ref[...])
        k32_ref[pl.ds(hd, tm, stride=HEADS), :] = kn
        k16_ref[:, sl] = kn.astype(BF16)

    zv = _dot(h, w_ref[:, 2 * width:3 * width])
    for hd in range(HEADS):
        sl = slice(hd * HEAD_DIM, (hd + 1) * HEAD_DIM)
        v32_ref[pl.ds(hd, tm, stride=HEADS), :] = zv[:, sl]
        if v_transposed:
            v16_ref[sl, :] = jnp.transpose(zv[:, sl]).astype(BF16)
    if not v_transposed:
        v16_ref[...] = zv.astype(BF16)

    zg = _dot(h, w_ref[:, 3 * width:4 * width])
    sg_ref[...] = jax.nn.silu(zg)

    zf = _dot(h, wf_ref[...])
    logf_ref[...] = jax.nn.log_sigmoid(zf[:, 0:HEADS] + bf_ref[...])


def _fox_inproj(x, g, w, wf, b_f, q_g, k_g, layer, n_layers, kv_prev, v_transposed):
    n, d = x.shape
    width = HEADS * HEAD_DIM
    tm = ROW_TILE
    first = kv_prev is None
    row = lambda cols: pl.BlockSpec((tm, cols), lambda i: (i, 0))
    if first:
        kv_spec = pl.BlockSpec((n_layers, tm * HEADS, HEAD_DIM), lambda i: (0, i, 0))
    else:
        kv_spec = pl.BlockSpec((None, tm * HEADS, HEAD_DIM), lambda i: (layer, i, 0))
    kv_shape = jax.ShapeDtypeStruct((n_layers, n * HEADS, HEAD_DIM), F32)
    if v_transposed:
        v16_shape = jax.ShapeDtypeStruct((n // tm, width, tm), BF16)
        v16_spec = pl.BlockSpec((None, width, tm), lambda i: (i, 0, 0))
    else:
        v16_shape = jax.ShapeDtypeStruct((n, width), BF16)
        v16_spec = row(width)
    out_shape = (
        jax.ShapeDtypeStruct((n, width), BF16),
        kv_shape,
        kv_shape,
        jax.ShapeDtypeStruct((n, width), BF16),
        v16_shape,
        jax.ShapeDtypeStruct((n, width), F32),
        jax.ShapeDtypeStruct((n, HEADS), F32),
    )
    in_specs = [row(d), _const_spec((1, d)), _const_spec(w.shape), _const_spec(wf.shape),
                _const_spec((1, HEADS)), _const_spec((1, HEAD_DIM)), _const_spec((1, HEAD_DIM))]
    args = [x, g.reshape(1, d), w, wf, b_f.reshape(1, HEADS), q_g.reshape(1, HEAD_DIM), k_g.reshape(1, HEAD_DIM)]
    aliases = {}
    if not first:
        in_specs += [pl.BlockSpec(memory_space=pl.ANY)] * 2
        aliases = {len(args): 1, len(args) + 1: 2}
        args += list(kv_prev)
    return pl.pallas_call(
        functools.partial(_fox_inproj_kernel, layer=layer, first=first, v_transposed=v_transposed),
        grid=(n // tm,),
        in_specs=in_specs,
        out_specs=(row(width), kv_spec, kv_spec, row(width), v16_spec, row(width), row(HEADS)),
        out_shape=out_shape,
        input_output_aliases=aliases,
        compiler_params=_params(1),
        name="fox_inproj",
    )(*args)


def _split3(x):
    hi = x.astype(BF16)
    r1 = x - hi.astype(F32)
    mid = r1.astype(BF16)
    lo = (r1 - mid.astype(F32)).astype(BF16)
    return hi, mid, lo


def _cumsum_kernel(x_ref, o_ref, *, chain_shift):
    rows, lb = x_ref.shape
    r = lax.broadcasted_iota(jnp.int32, (lb, lb), 0)
    c = lax.broadcasted_iota(jnp.int32, (lb, lb), 1)
    upper = (r <= c).astype(BF16)
    hi, mid, lo = _split3(x_ref[...])
    cs = _dot(hi, upper) + _dot(mid, upper) + _dot(lo, upper)
    if chain_shift:
        rb = lax.broadcasted_iota(jnp.int32, (rows, rows), 0)
        cb = lax.broadcasted_iota(jnp.int32, (rows, rows), 1)
        tot = jnp.broadcast_to(cs[:, lb - 1:lb], (rows, rows))
        tot_row = jnp.sum(jnp.where(rb == cb, tot, 0.0), axis=0, keepdims=True)
        earlier = (cb < rb) & (lax.shift_right_logical(cb, chain_shift) == lax.shift_right_logical(rb, chain_shift))
        offs = jnp.sum(jnp.where(earlier, jnp.broadcast_to(tot_row, (rows, rows)), 0.0),
                       axis=1, keepdims=True)
        cs = cs + offs
    o_ref[...] = cs


def _cumsum_lanes(x, chained):
    bsz, nh, nb, lb = x.shape
    chain_shift = (nb.bit_length() - 1) if chained else 0
    assert nb == 1 << (nb.bit_length() - 1)
    spec = pl.BlockSpec((None, nh * nb, lb), lambda b: (b, 0, 0))
    out = pl.pallas_call(
        functools.partial(_cumsum_kernel, chain_shift=chain_shift),
        grid=(bsz,),
        in_specs=[spec],
        out_specs=spec,
        out_shape=jax.ShapeDtypeStruct((bsz, nh * nb, lb), F32),
        compiler_params=_params(1),
        name="logf_cumsum",
    )(x.reshape(bsz, nh * nb, lb))
    return out.reshape(x.shape)


def _row_to_col(row, eye):
    n = row.shape[1]
    return jnp.sum(jnp.where(eye, jnp.broadcast_to(row, (n, n)), 0.0), axis=1, keepdims=True)


def _prompt_attn_kernel(q_ref, k_ref, vt_ref, sg_ref, f_ref, og_ref, m_s, l_s, acc_s, fk_s):
    t = ATTN_BLOCK
    hp, nblk = f_ref.shape[0], f_ref.shape[1]
    heads = range(hp)
    hs = lambda h: slice(h * HEAD_DIM, (h + 1) * HEAD_DIM)
    key_i = lax.broadcasted_iota(jnp.int32, (t, t), 0)
    qry_i = lax.broadcasted_iota(jnp.int32, (t, t), 1)
    causal = key_i <= qry_i
    for h in heads:
        for blk in range(nblk):
            fk_s[h, blk] = _row_to_col(f_ref[h, blk] * LOG2E, key_i == qry_i)

    def step(q, fq, k_of, vt_of, fk_of, masked):
        ut = [_dot_nt(k_of(h), q[h]) - fk_of(h) for h in heads]
        if masked:
            ut = [jnp.where(causal, x, NEG_INF) for x in ut]
        m_old = [m_s[h] for h in heads]
        m_new = [jnp.maximum(m_old[h], jnp.max(ut[h], axis=0, keepdims=True) + fq[h]) for h in heads]
        pt = [jnp.exp2(ut[h] - (m_new[h] - fq[h])) for h in heads]
        alpha = [jnp.exp2(m_old[h] - m_new[h]) for h in heads]
        for h in heads:
            m_s[h] = m_new[h]
            l_s[h] = alpha[h] * l_s[h] + jnp.sum(pt[h], axis=0, keepdims=True)
        for h in heads:
            acc_s[h] = alpha[h] * acc_s[h] + _dot(vt_of(h), pt[h].astype(BF16))

    for qi in range(nblk):
        qs = slice(qi * t, (qi + 1) * t)
        q = [q_ref[qs, hs(h)] for h in heads]
        fq = [f_ref[h, qi] * LOG2E for h in heads]
        m_s[...] = jnp.full(m_s.shape, NEG_INF, F32)
        l_s[...] = jnp.zeros(l_s.shape, F32)
        acc_s[...] = jnp.zeros(acc_s.shape, F32)

        def body(ki, carry, q=q, fq=fq):
            start = pl.multiple_of(ki * t, t)
            step(q, fq, lambda h: k_ref[pl.ds(start, t), hs(h)], lambda h: vt_ref[ki, hs(h), :],
                 lambda h: fk_s[h, ki], False)
            return carry

        lax.fori_loop(0, qi, body, 0)
        step(q, fq, lambda h: k_ref[qs, hs(h)], lambda h: vt_ref[qi, hs(h), :], lambda h: fk_s[h, qi], True)
        for h in heads:
            o = jnp.transpose(acc_s[h] / l_s[h])
            og_ref[qs, hs(h)] = (o * sg_ref[qs, hs(h)]).astype(BF16)


def _prompt_attn(q16, k16, vt16, sg, fcum, bsz, s_len):
    t = ATTN_BLOCK
    hp = HEADS_PER_STEP
    nblk = s_len // t
    n, width = q16.shape
    blk = pl.BlockSpec((s_len, hp * HEAD_DIM), lambda b, g: (b, g))
    return pl.pallas_call(
        _prompt_attn_kernel,
        grid=(bsz, HEADS // hp),
        in_specs=[blk, blk,
                  pl.BlockSpec((nblk, hp * HEAD_DIM, t), lambda b, g: (b, g, 0)),
                  blk,
                  pl.BlockSpec((None, hp, nblk, 1, t), lambda b, g: (b, g, 0, 0, 0))],
        out_specs=blk,
        out_shape=jax.ShapeDtypeStruct((n, width), BF16),
        scratch_shapes=[pltpu.VMEM((hp, 1, t), F32), pltpu.VMEM((hp, 1, t), F32),
                        pltpu.VMEM((hp, HEAD_DIM, t), F32), pltpu.VMEM((hp, nblk, t, 1), F32)],
        compiler_params=_params(2),
        name="prompt_attn",
    )(q16, k16, vt16, sg, fcum)


def _sample_attn_kernel(q_ref, ck_ref, cv_ref, kn_ref, vn_ref, sg_ref, fp_ref, fl_ref, fn_ref,
                        og_ref, m_ref, l_ref, acc_ref, fq_ref):
    ki = pl.program_id(1)
    last = pl.num_programs(1) - 1
    t = q_ref.shape[0]
    tk = fp_ref.shape[1]
    heads = range(HEADS)
    hs = lambda hd: slice(hd * HEAD_DIM, (hd + 1) * HEAD_DIM)
    rows = lax.broadcasted_iota(jnp.int32, (t, t), 0)
    cols = lax.broadcasted_iota(jnp.int32, (t, t), 1)

    @pl.when(ki == 0)
    def _():
        m_ref[...] = jnp.full(m_ref.shape, NEG_INF, F32)
        l_ref[...] = jnp.zeros(l_ref.shape, F32)
        acc_ref[...] = jnp.zeros(acc_ref.shape, F32)
        f_new = (fl_ref[...] + fn_ref[...]) * LOG2E
        for hd in heads:
            fq_ref[hd] = _row_to_col(f_new[hd:hd + 1, :], rows == cols)

    def update(with_new):
        f_new = (fl_ref[...] + fn_ref[...]) * LOG2E
        f_past = fp_ref[...] * LOG2E
        s = []
        for hd in heads:
            q = q_ref[:, hs(hd)]
            k = ck_ref[pl.ds(hd, tk, stride=HEADS), :].astype(BF16)
            segs = [_dot_nt(q, k) - f_past[hd:hd + 1, :]]
            if with_new:
                sn = _dot_nt(q, kn_ref[:, hs(hd)]) - f_new[hd:hd + 1, :]
                segs.append(jnp.where(cols <= rows, sn, NEG_INF))
            s.append(segs)
        fq = [fq_ref[hd] for hd in heads]
        m_old = [m_ref[hd] for hd in heads]
        m_new = [jnp.maximum(m_old[hd], functools.reduce(
            jnp.maximum, [jnp.max(x, axis=1, keepdims=True) for x in s[hd]]) + fq[hd]) for hd in heads]
        p = [[jnp.exp2(x - (m_new[hd] - fq[hd])) for x in s[hd]] for hd in heads]
        alpha = [jnp.exp2(m_old[hd] - m_new[hd]) for hd in heads]
        l_new = [alpha[hd] * l_ref[hd] + sum(jnp.sum(x, axis=1, keepdims=True) for x in p[hd]) for hd in heads]
        for hd in heads:
            v = cv_ref[pl.ds(hd, tk, stride=HEADS), :].astype(BF16)
            pv = _dot(p[hd][0].astype(BF16), v)
            if with_new:
                pv = pv + _dot(p[hd][1].astype(BF16), vn_ref[:, hs(hd)])
            acc = alpha[hd] * acc_ref[hd] + pv
            if with_new:
                og_ref[:, hs(hd)] = (acc / l_new[hd] * sg_ref[:, hs(hd)]).astype(BF16)
            else:
                m_ref[hd] = m_new[hd]
                l_ref[hd] = l_new[hd]
                acc_ref[hd] = acc

    @pl.when(ki != last)
    def _():
        update(False)

    @pl.when(ki == last)
    def _():
        update(True)


def _sample_attn(q16, cache_k, cache_v, layer, k16, v16, sg, f_past, f_last, f_new, bsz, t):
    n, width = q16.shape
    p_len = f_past.shape[2]
    tk = SAMPLE_KV_BLOCK
    new = pl.BlockSpec((t, width), lambda b, k: (b, 0))
    past = pl.BlockSpec((None, None, tk * HEADS, HEAD_DIM), lambda b, k: (layer, b, k, 0))
    return pl.pallas_call(
        _sample_attn_kernel,
        grid=(bsz, p_len // tk),
        in_specs=[new, past, past, new, new, new,
                  pl.BlockSpec((None, HEADS, tk), lambda b, k: (b, 0, k)),
                  pl.BlockSpec((None, HEADS, 1), lambda b, k: (b, 0, 0)),
                  pl.BlockSpec((None, HEADS, t), lambda b, k: (b, 0, 0))],
        out_specs=new,
        out_shape=jax.ShapeDtypeStruct((n, width), BF16),
        scratch_shapes=[pltpu.VMEM((HEADS, t, 1), F32), pltpu.VMEM((HEADS, t, 1), F32),
                        pltpu.VMEM((HEADS, t, HEAD_DIM), F32), pltpu.VMEM((HEADS, t, 1), F32)],
        compiler_params=_params(2),
        name="sample_attn",
    )(q16, cache_k, cache_v, k16, v16, sg, f_past, f_last, f_new)


def _outproj_kernel(x_ref, a_ref, w_ref, o_ref):
    o_ref[...] = x_ref[...] + _dot(a_ref[...], w_ref[...])


def _outproj(x, a, w):
    n, d = x.shape
    tm = OUTPROJ_ROW_TILE
    return pl.pallas_call(
        _outproj_kernel,
        grid=(n // tm,),
        in_specs=[pl.BlockSpec((tm, d), lambda i: (i, 0)),
                  pl.BlockSpec((tm, a.shape[1]), lambda i: (i, 0)),
                  _const_spec(w.shape)],
        out_specs=pl.BlockSpec((tm, d), lambda i: (i, 0)),
        out_shape=jax.ShapeDtypeStruct((n, d), F32),
        compiler_params=_params(1),
        name="outproj",
    )(x, a, w)


def _gmlp_kernel(x_ref, g_ref, win_ref, vg_ref, mix_ref, bias_ref, wout_ref, *rest, emit_v):
    if emit_v:
        o_ref, vn_ref, a_ref = rest
    else:
        o_ref, a_ref = rest
    tm = x_ref.shape[0]
    bw = wout_ref.shape[0]
    gd = bw // GROUPS
    x = x_ref[...]
    h = _rmsnorm(x, g_ref[...]).astype(BF16)
    v = _dot(h, win_ref[:, bw:bw + gd])
    for gi in range(GROUPS):
        sl = slice(gi * gd, (gi + 1) * gd)
        u = _dot(h, win_ref[:, sl])
        gate = _dot(h, win_ref[:, 2 * bw + gi * gd:2 * bw + (gi + 1) * gd])
        vn = _rmsnorm(v, vg_ref[:, sl])
        if gi + 1 < GROUPS:
            v = _dot(h, win_ref[:, bw + (gi + 1) * gd:bw + (gi + 2) * gd])
        if emit_v:
            vn_ref[:, sl] = vn
        vn16 = vn.astype(BF16)
        mixed = jnp.concatenate(
            [_dot(mix_ref[gi], vn16[ci * MLP_CHUNK:(ci + 1) * MLP_CHUNK, :]) + bias_ref[:, gi:gi + 1]
             for ci in range(tm // MLP_CHUNK)], axis=0)
        a_ref[:, sl] = (u * mixed * jax.nn.silu(gate)).astype(BF16)
    o_ref[...] = x + _dot(a_ref[...], wout_ref[...])


def _gmlp_layer(x, g, win, vg, mix, bias, wout, emit_v):
    n, d = x.shape
    bw = wout.shape[0]
    tm = GMLP_ROW_TILE
    row = lambda cols: pl.BlockSpec((tm, cols), lambda i: (i, 0))
    out_shape = [jax.ShapeDtypeStruct((n, d), F32)]
    out_specs = [row(d)]
    if emit_v:
        out_shape.append(jax.ShapeDtypeStruct((n, bw), F32))
        out_specs.append(row(bw))
    res = pl.pallas_call(
        functools.partial(_gmlp_kernel, emit_v=emit_v),
        grid=(n // tm,),
        in_specs=[row(d), _const_spec((1, d)), _const_spec(win.shape), _const_spec((1, bw)),
                  _const_spec(mix.shape), _const_spec(bias.shape), _const_spec(wout.shape)],
        out_specs=tuple(out_specs),
        out_shape=tuple(out_shape),
        scratch_shapes=[pltpu.VMEM((tm, bw), BF16)],
        compiler_params=_params(1),
        name="gmlp_layer",
    )(x, g.reshape(1, d), win, vg.reshape(1, bw), mix, bias, wout)
    return res if emit_v else res[0]


def _sgu_mask():
    c = jnp.arange(MLP_CHUNK) // STREAM_CHUNK
    return c[None, :] <= c[:, None]


def kernel(x_prompt, x_sample, cache_k, cache_v, cache_logf, norm_g, w_in_a, b_f, q_g, k_g, w_out_a,
           w_in_b, v_g, ws, bs, w_out_b):
    bsz, s_len, d = x_prompt.shape
    dbsz, t_new, _ = x_sample.shape
    p_len = cache_k.shape[2]
    width = HEADS * HEAD_DIM
    depth = norm_g.shape[0]
    assert ROW_TILE == ATTN_BLOCK and HEADS % HEADS_PER_STEP == 0
    assert s_len % ATTN_BLOCK == 0 and s_len % CUMSUM_BLOCK == 0 and p_len % CUMSUM_BLOCK == 0
    assert p_len % SAMPLE_KV_BLOCK == 0 and MLP_CHUNK == 2 * t_new and GMLP_ROW_TILE % MLP_CHUNK == 0

    xp = x_prompt.reshape(bsz * s_len, d)
    xs = x_sample.reshape(dbsz * t_new, d)
    n_a = w_in_a.shape[0]
    ck = cache_k.reshape(n_a, dbsz, p_len * HEADS, HEAD_DIM)
    cv = cache_v.reshape(n_a, dbsz, p_len * HEADS, HEAD_DIM)
    kv_p, kv_s = None, None
    lp_l, ls_l, sgu_l = [], [], []
    for i in range(depth):
        j = i // 2
        if i % 2 == 0:
            w = w_in_a[j][:, :4 * width].astype(BF16)
            wf = jnp.pad(w_in_a[j][:, 4 * width:], ((0, 0), (0, HEAD_DIM - HEADS))).astype(BF16)
            wo = w_out_a[j].astype(BF16)

            q16, k_all, v_all, k16, v16, sg, logf = _fox_inproj(xp, norm_g[i], w, wf, b_f[j], q_g[j], k_g[j],
                                                                j, n_a, kv_p, v_transposed=True)
            kv_p = (k_all, v_all)
            nb = s_len // CUMSUM_BLOCK
            lt = jnp.transpose(logf.reshape(bsz, s_len, HEADS), (0, 2, 1))
            fcum = _cumsum_lanes(lt.reshape(bsz, HEADS, nb, CUMSUM_BLOCK), chained=True)
            fcum = fcum.reshape(bsz, HEADS, s_len // ATTN_BLOCK, 1, ATTN_BLOCK)
            og = _prompt_attn(q16, k16, v16, sg, fcum, bsz, s_len)
            xp = _outproj(xp, og, wo)
            lp_l.append(logf.reshape(bsz, s_len, HEADS))

            q16, k_all, v_all, k16, v16, sg, logf = _fox_inproj(xs, norm_g[i], w, wf, b_f[j], q_g[j], k_g[j],
                                                                j, n_a, kv_s, v_transposed=False)
            kv_s = (k_all, v_all)
            nbp = p_len // CUMSUM_BLOCK
            cl = jnp.transpose(cache_logf[j], (0, 2, 1)).reshape(dbsz, HEADS, nbp, CUMSUM_BLOCK)
            f_past = _cumsum_lanes(cl, chained=True).reshape(dbsz, HEADS, p_len)
            lt = jnp.transpose(logf.reshape(dbsz, t_new, HEADS), (0, 2, 1))
            f_new = _cumsum_lanes(lt.reshape(1, HEADS, dbsz, t_new), chained=False)
            f_new = f_new.reshape(dbsz, HEADS, t_new)
            og = _sample_attn(q16, ck, cv, j, k16, v16, sg, f_past, f_past[:, :, p_len - 1:], f_new, dbsz, t_new)
            xs = _outproj(xs, og, wo)
            ls_l.append(logf.reshape(dbsz, t_new, HEADS))
        else:
            win = w_in_b[j].astype(BF16)
            wout = w_out_b[j].astype(BF16)
            bw = wout.shape[0]
            mix_p = (ws[j] * _sgu_mask()[None]).astype(BF16)
            bias_p = jnp.transpose(bs[j])
            xp = _gmlp_layer(xp, norm_g[i], win, v_g[j], mix_p, bias_p, wout, emit_v=False)
            a = (ws[j] * _sgu_mask()[None])[:, :t_new, :t_new]
            z = jnp.zeros_like(a)
            mix_s = jnp.concatenate([jnp.concatenate([a, z], axis=2),
                                     jnp.concatenate([z, a], axis=2)], axis=1).astype(BF16)
            bias_s = jnp.transpose(jnp.concatenate([bs[j][:, :t_new], bs[j][:, :t_new]], axis=1))
            xs, sv = _gmlp_layer(xs, norm_g[i], win, v_g[j], mix_s, bias_s, wout, emit_v=True)
            sgu_l.append(sv.reshape(dbsz, t_new, bw))
    kv5 = lambda a, nb_, t_: a.reshape(n_a, nb_, t_, HEADS, HEAD_DIM)
    return (xp.reshape(bsz, s_len, d), xs.reshape(dbsz, t_new, d),
            kv5(kv_p[0], bsz, s_len), kv5(kv_p[1], bsz, s_len), jnp.stack(lp_l),
            kv5(kv_s[0], dbsz, t_new), kv5(kv_s[1], dbsz, t_new), jnp.stack(ls_l), jnp.stack(sgu_l))
```

```python
import functools

import jax
import jax.numpy as jnp
from jax import lax
from jax.experimental import pallas as pl
from jax.experimental.pallas import tpu as pltpu

F32 = jnp.float32
BF16 = jnp.bfloat16

EPS = 1e-6
NEG_INF = -1e30
LOG2E = 1.4426950408889634

HEADS = 8
HEAD_DIM = 128
Q_PRESCALE = HEAD_DIM ** -0.5 * LOG2E
GROUPS = 8
MLP_CHUNK = 128
STREAM_CHUNK = 64

V7X_VMEM_BYTES = 64 * 1024 * 1024
VMEM_LIMIT_BYTES = V7X_VMEM_BYTES - 8 * 1024 * 1024

ROW_TILE = 512
GMLP_ROW_TILE = 512
ATTN_BLOCK = 512
HEADS_PER_STEP = 4
SAMPLE_KV_BLOCK = 2048
CUMSUM_BLOCK = 256
CUMSUM_ROWS = 512


def _params(n_axes):
    return pltpu.CompilerParams(dimension_semantics=("arbitrary",) * n_axes,
                                vmem_limit_bytes=VMEM_LIMIT_BYTES)


def _const_spec(shape):
    nd = len(shape)
    return pl.BlockSpec(shape, lambda *_: (0,) * nd, pipeline_mode=pl.Buffered(1))


def _layer_spec(stacked_shape, layer):
    nd = len(stacked_shape) - 1
    return pl.BlockSpec((None,) + tuple(stacked_shape[1:]), lambda *_: (layer,) + (0,) * nd,
                        pipeline_mode=pl.Buffered(1))


def _rmsnorm(x, g):
    ms = jnp.mean(x * x, axis=-1, keepdims=True)
    return x * lax.rsqrt(ms + EPS) * g


def _dot(a, b):
    return jnp.dot(a, b, preferred_element_type=F32)


def _dot_nt(a, b):
    return lax.dot_general(a, b, (((1,), (1,)), ((), ())), preferred_element_type=F32)


def _fox_inproj_kernel(x_ref, g_ref, w_ref, wf_ref, bf_ref, qg_ref, kg_ref, *rest, layer, first, v_transposed):
    q16_ref, k32_ref, v32_ref, k16_ref, v16_ref, sg_ref, logf_ref = rest[-7:]
    if first:
        for slot in range(k32_ref.shape[0]):
            if slot != layer:
                k32_ref[slot] = jnp.zeros(k32_ref.shape[1:], F32)
                v32_ref[slot] = jnp.zeros(v32_ref.shape[1:], F32)
        k32_ref = k32_ref.at[layer]
        v32_ref = v32_ref.at[layer]
    tm = x_ref.shape[0]
    width = HEADS * HEAD_DIM
    h = _rmsnorm(x_ref[...], g_ref[...]).astype(BF16)

    zq = _dot(h, w_ref[:, 0:width])
    for hd in range(HEADS):
        sl = slice(hd * HEAD_DIM, (hd + 1) * HEAD_DIM)
        q16_ref[:, sl] = (_rmsnorm(zq[:, sl], qg_ref[...]) * Q_PRESCALE).astype(BF16)

    zk = _dot(h, w_ref[:, width:2 * width])
    for hd in range(HEADS):
        sl = slice(hd * HEAD_DIM, (hd + 1) * HEAD_DIM)
        kn = _rmsnorm(zk[:, sl], kg_ref[...])
        k32_ref[pl.ds(hd, tm, stride=HEADS), :] = kn
        k16_ref[:, sl] = kn.astype(BF16)

    zv = _dot(h, w_ref[:, 2 * width:3 * width])
    for hd in range(HEADS):
        sl = slice(hd * HEAD_DIM, (hd + 1) * HEAD_DIM)
        v32_ref[pl.ds(hd, tm, stride=HEADS), :] = zv[:, sl]
        if v_transposed:
            v16_ref[sl, :] = jnp.transpose(zv[:, sl]).astype(BF16)
    if not v_transposed:
        v16_ref[...] = zv.astype(BF16)

    zg = _dot(h, w_ref[:, 3 * width:4 * width])
    sg_ref[...] = jax.nn.silu(zg)

    zf = _dot(h, wf_ref[...])
    logf_ref[...] = jax.nn.log_sigmoid(zf[:, 0:HEADS] + bf_ref[...])


def _fox_inproj(x, g, w, wf, b_f, q_g, k_g, layer, n_layers, kv_prev, v_transposed):
    n, d = x.shape
    width = HEADS * HEAD_DIM
    tm = ROW_TILE
    first = kv_prev is None
    row = lambda cols: pl.BlockSpec((tm, cols), lambda i: (i, 0))
    if first:
        kv_spec = pl.BlockSpec((n_layers, tm * HEADS, HEAD_DIM), lambda i: (0, i, 0))
    else:
        kv_spec = pl.BlockSpec((None, tm * HEADS, HEAD_DIM), lambda i: (layer, i, 0))
    kv_shape = jax.ShapeDtypeStruct((n_layers, n * HEADS, HEAD_DIM), F32)
    if v_transposed:
        v16_shape = jax.ShapeDtypeStruct((n // tm, width, tm), BF16)
        v16_spec = pl.BlockSpec((None, width, tm), lambda i: (i, 0, 0))
    else:
        v16_shape = jax.ShapeDtypeStruct((n, width), BF16)
        v16_spec = row(width)
    out_shape = (
        jax.ShapeDtypeStruct((n, width), BF16),
        kv_shape,
        kv_shape,
        jax.ShapeDtypeStruct((n, width), BF16),
        v16_shape,
        jax.ShapeDtypeStruct((n, width), F32),
        jax.ShapeDtypeStruct((n, HEADS), F32),
    )
    in_specs = [row(d), _const_spec((1, d)), _layer_spec(w.shape, layer), _layer_spec(wf.shape, layer),
                _const_spec((1, HEADS)), _const_spec((1, HEAD_DIM)), _const_spec((1, HEAD_DIM))]
    args = [x, g.reshape(1, d), w, wf, b_f.reshape(1, HEADS), q_g.reshape(1, HEAD_DIM), k_g.reshape(1, HEAD_DIM)]
    aliases = {}
    if not first:
        in_specs += [pl.BlockSpec(memory_space=pl.ANY)] * 2
        aliases = {len(args): 1, len(args) + 1: 2}
        args += list(kv_prev)
    return pl.pallas_call(
        functools.partial(_fox_inproj_kernel, layer=layer, first=first, v_transposed=v_transposed),
        grid=(n // tm,),
        in_specs=in_specs,
        out_specs=(row(width), kv_spec, kv_spec, row(width), v16_spec, row(width), row(HEADS)),
        out_shape=out_shape,
        input_output_aliases=aliases,
        compiler_params=_params(1),
        name="fox_inproj",
    )(*args)


def _split3(x):
    hi = x.astype(BF16)
    r1 = x - hi.astype(F32)
    mid = r1.astype(BF16)
    lo = (r1 - mid.astype(F32)).astype(BF16)
    return hi, mid, lo


def _cumsum_kernel(x_ref, o_ref, *, chain_shift):
    rows, lb = x_ref.shape
    r = lax.broadcasted_iota(jnp.int32, (lb, lb), 0)
    c = lax.broadcasted_iota(jnp.int32, (lb, lb), 1)
    upper = (r <= c).astype(BF16)
    hi, mid, lo = _split3(x_ref[...])
    cs = _dot(hi, upper) + _dot(mid, upper) + _dot(lo, upper)
    if chain_shift:
        rb = lax.broadcasted_iota(jnp.int32, (rows, rows), 0)
        cb = lax.broadcasted_iota(jnp.int32, (rows, rows), 1)
        tot = jnp.broadcast_to(cs[:, lb - 1:lb], (rows, rows))
        tot_row = jnp.sum(jnp.where(rb == cb, tot, 0.0), axis=0, keepdims=True)
        earlier = (cb < rb) & (lax.shift_right_logical(cb, chain_shift) == lax.shift_right_logical(rb, chain_shift))
        offs = jnp.sum(jnp.where(earlier, jnp.broadcast_to(tot_row, (rows, rows)), 0.0),
                       axis=1, keepdims=True)
        cs = cs + offs
    o_ref[...] = cs


def _cumsum_lanes(x, chained):
    bsz, nh, nb, lb = x.shape
    chain_shift = (nb.bit_length() - 1) if chained else 0
    assert nb == 1 << (nb.bit_length() - 1)
    total = bsz * nh * nb
    rows = min(total, CUMSUM_ROWS)
    assert total % rows == 0 and rows % nb == 0
    spec = pl.BlockSpec((None, rows, lb), lambda b: (b, 0, 0))
    out = pl.pallas_call(
        functools.partial(_cumsum_kernel, chain_shift=chain_shift),
        grid=(total // rows,),
        in_specs=[spec],
        out_specs=spec,
        out_shape=jax.ShapeDtypeStruct((total // rows, rows, lb), F32),
        compiler_params=_params(1),
        name="logf_cumsum",
    )(x.reshape(total // rows, rows, lb))
    return out.reshape(x.shape)


def _row_to_col(row, eye):
    n = row.shape[1]
    return jnp.sum(jnp.where(eye, jnp.broadcast_to(row, (n, n)), 0.0), axis=1, keepdims=True)


def _prompt_attn_kernel(q_ref, k_ref, vt_ref, sg_ref, f_ref, og_ref, m_s, l_s, acc_s, fk_s):
    t = ATTN_BLOCK
    hp, nblk = f_ref.shape[0], f_ref.shape[1]
    heads = range(hp)
    hs = lambda h: slice(h * HEAD_DIM, (h + 1) * HEAD_DIM)
    key_i = lax.broadcasted_iota(jnp.int32, (t, t), 0)
    qry_i = lax.broadcasted_iota(jnp.int32, (t, t), 1)
    causal = key_i <= qry_i
    for h in heads:
        for blk in range(nblk):
            fk_s[h, blk] = _row_to_col(f_ref[h, blk] * LOG2E, key_i == qry_i)

    def step(q, fq, k_of, vt_of, fk_of, masked):
        ut = [_dot_nt(k_of(h), q[h]) - fk_of(h) for h in heads]
        if masked:
            ut = [jnp.where(causal, x, NEG_INF) for x in ut]
        m_old = [m_s[h] for h in heads]
        m_new = [jnp.maximum(m_old[h], jnp.max(ut[h], axis=0, keepdims=True) + fq[h]) for h in heads]
        pt = [jnp.exp2(ut[h] - (m_new[h] - fq[h])) for h in heads]
        alpha = [jnp.exp2(m_old[h] - m_new[h]) for h in heads]
        for h in heads:
            m_s[h] = m_new[h]
            l_s[h] = alpha[h] * l_s[h] + jnp.sum(pt[h], axis=0, keepdims=True)
        for h in heads:
            acc_s[h] = alpha[h] * acc_s[h] + _dot(vt_of(h), pt[h].astype(BF16))

    for qi in range(nblk):
        qs = slice(qi * t, (qi + 1) * t)
        q = [q_ref[qs, hs(h)] for h in heads]
        fq = [f_ref[h, qi] * LOG2E for h in heads]
        m_s[...] = jnp.full(m_s.shape, NEG_INF, F32)
        l_s[...] = jnp.zeros(l_s.shape, F32)
        acc_s[...] = jnp.zeros(acc_s.shape, F32)

        def body(ki, carry, q=q, fq=fq):
            start = pl.multiple_of(ki * t, t)
            step(q, fq, lambda h: k_ref[pl.ds(start, t), hs(h)], lambda h: vt_ref[ki, hs(h), :],
                 lambda h: fk_s[h, ki], False)
            return carry

        lax.fori_loop(0, qi, body, 0)
        step(q, fq, lambda h: k_ref[qs, hs(h)], lambda h: vt_ref[qi, hs(h), :], lambda h: fk_s[h, qi], True)
        for h in heads:
            o = jnp.transpose(acc_s[h] / l_s[h])
            og_ref[qs, hs(h)] = (o * sg_ref[qs, hs(h)]).astype(BF16)


def _prompt_attn(q16, k16, vt16, sg, fcum, bsz, s_len):
    t = ATTN_BLOCK
    hp = HEADS_PER_STEP
    nblk = s_len // t
    n, width = q16.shape
    blk = pl.BlockSpec((s_len, hp * HEAD_DIM), lambda b, g: (b, g))
    return pl.pallas_call(
        _prompt_attn_kernel,
        grid=(bsz, HEADS // hp),
        in_specs=[blk, blk,
                  pl.BlockSpec((nblk, hp * HEAD_DIM, t), lambda b, g: (b, g, 0)),
                  blk,
                  pl.BlockSpec((None, hp, nblk, 1, t), lambda b, g: (b, g, 0, 0, 0))],
        out_specs=blk,
        out_shape=jax.ShapeDtypeStruct((n, width), BF16),
        scratch_shapes=[pltpu.VMEM((hp, 1, t), F32), pltpu.VMEM((hp, 1, t), F32),
                        pltpu.VMEM((hp, HEAD_DIM, t), F32), pltpu.VMEM((hp, nblk, t, 1), F32)],
        compiler_params=_params(2),
        name="prompt_attn",
    )(q16, k16, vt16, sg, fcum)


def _sample_attn_kernel(q_ref, ck_ref, cv_ref, kn_ref, vn_ref, sg_ref, fp_ref, fl_ref, fn_ref,
                        og_ref, m_ref, l_ref, acc_ref, fq_ref):
    ki = pl.program_id(1)
    last = pl.num_programs(1) - 1
    t = q_ref.shape[0]
    tk = fp_ref.shape[1]
    heads = range(HEADS)
    hs = lambda hd: slice(hd * HEAD_DIM, (hd + 1) * HEAD_DIM)
    rows = lax.broadcasted_iota(jnp.int32, (t, t), 0)
    cols = lax.broadcasted_iota(jnp.int32, (t, t), 1)

    @pl.when(ki == 0)
    def _():
        m_ref[...] = jnp.full(m_ref.shape, NEG_INF, F32)
        l_ref[...] = jnp.zeros(l_ref.shape, F32)
        acc_ref[...] = jnp.zeros(acc_ref.shape, F32)
        f_new = (fl_ref[...] + fn_ref[...]) * LOG2E
        for hd in heads:
            fq_ref[hd] = _row_to_col(f_new[hd:hd + 1, :], rows == cols)

    def update(with_new):
        f_new = (fl_ref[...] + fn_ref[...]) * LOG2E
        f_past = fp_ref[...] * LOG2E
        s = []
        for hd in heads:
            q = q_ref[:, hs(hd)]
            k = ck_ref[pl.ds(hd, tk, stride=HEADS), :].astype(BF16)
            segs = [_dot_nt(q, k) - f_past[hd:hd + 1, :]]
            if with_new:
                sn = _dot_nt(q, kn_ref[:, hs(hd)]) - f_new[hd:hd + 1, :]
                segs.append(jnp.where(cols <= rows, sn, NEG_INF))
            s.append(segs)
        fq = [fq_ref[hd] for hd in heads]
        m_old = [m_ref[hd] for hd in heads]
        m_new = [jnp.maximum(m_old[hd], functools.reduce(
            jnp.maximum, [jnp.max(x, axis=1, keepdims=True) for x in s[hd]]) + fq[hd]) for hd in heads]
        p = [[jnp.exp2(x - (m_new[hd] - fq[hd])) for x in s[hd]] for hd in heads]
        alpha = [jnp.exp2(m_old[hd] - m_new[hd]) for hd in heads]
        l_new = [alpha[hd] * l_ref[hd] + sum(jnp.sum(x, axis=1, keepdims=True) for x in p[hd]) for hd in heads]
        for hd in heads:
            v = cv_ref[pl.ds(hd, tk, stride=HEADS), :].astype(BF16)
            pv = _dot(p[hd][0].astype(BF16), v)
            if with_new:
                pv = pv + _dot(p[hd][1].astype(BF16), vn_ref[:, hs(hd)])
            acc = alpha[hd] * acc_ref[hd] + pv
            if with_new:
                og_ref[:, hs(hd)] = (acc / l_new[hd] * sg_ref[:, hs(hd)]).astype(BF16)
            else:
                m_ref[hd] = m_new[hd]
                l_ref[hd] = l_new[hd]
                acc_ref[hd] = acc

    @pl.when(ki != last)
    def _():
        update(False)

    @pl.when(ki == last)
    def _():
        update(True)


def _sample_attn(q16, cache_k, cache_v, layer, k16, v16, sg, f_past, f_last, f_new, bsz, t):
    n, width = q16.shape
    p_len = f_past.shape[2]
    tk = SAMPLE_KV_BLOCK
    new = pl.BlockSpec((t, width), lambda b, k: (b, 0))
    past = pl.BlockSpec((None, None, tk * HEADS, HEAD_DIM), lambda b, k: (layer, b, k, 0))
    return pl.pallas_call(
        _sample_attn_kernel,
        grid=(bsz, p_len // tk),
        in_specs=[new, past, past, new, new, new,
                  pl.BlockSpec((None, HEADS, tk), lambda b, k: (b, 0, k)),
                  pl.BlockSpec((None, HEADS, 1), lambda b, k: (b, 0, 0)),
                  pl.BlockSpec((None, HEADS, t), lambda b, k: (b, 0, 0))],
        out_specs=new,
        out_shape=jax.ShapeDtypeStruct((n, width), BF16),
        scratch_shapes=[pltpu.VMEM((HEADS, t, 1), F32), pltpu.VMEM((HEADS, t, 1), F32),
                        pltpu.VMEM((HEADS, t, HEAD_DIM), F32), pltpu.VMEM((HEADS, t, 1), F32)],
        compiler_params=_params(2),
        name="sample_attn",
    )(q16, cache_k, cache_v, k16, v16, sg, f_past, f_last, f_new)


def _gmlp_kernel(x_ref, ap_ref, wp_ref, g_ref, win_ref, vg_ref, mix_ref, bias_ref, wout_ref, *rest, emit_v):
    if emit_v:
        o_ref, vn_ref, a_ref = rest
    else:
        o_ref, a_ref = rest
    tm = x_ref.shape[0]
    bw = wout_ref.shape[0]
    gd = bw // GROUPS
    x = x_ref[...] + _dot(ap_ref[...], wp_ref[...])
    h = _rmsnorm(x, g_ref[...]).astype(BF16)
    v = _dot(h, win_ref[:, bw:bw + gd])
    for gi in range(GROUPS):
        sl = slice(gi * gd, (gi + 1) * gd)
        u = _dot(h, win_ref[:, sl])
        gate = _dot(h, win_ref[:, 2 * bw + gi * gd:2 * bw + (gi + 1) * gd])
        vn = _rmsnorm(v, vg_ref[:, sl])
        if gi + 1 < GROUPS:
            v = _dot(h, win_ref[:, bw + (gi + 1) * gd:bw + (gi + 2) * gd])
        if emit_v:
            vn_ref[:, sl] = vn
        vn16 = vn.astype(BF16)
        mixed = jnp.concatenate(
            [_dot(mix_ref[gi], vn16[ci * MLP_CHUNK:(ci + 1) * MLP_CHUNK, :]) + bias_ref[:, gi:gi + 1]
             for ci in range(tm // MLP_CHUNK)], axis=0)
        a_ref[:, sl] = (u * mixed * jax.nn.silu(gate)).astype(BF16)
    o_ref[...] = x + _dot(a_ref[...], wout_ref[...])


def _gmlp_layer(x, a_prev, w_prev, g, win, vg, mix, bias, wout, layer, emit_v):
    n, d = x.shape
    bw = wout.shape[1]
    tm = GMLP_ROW_TILE
    row = lambda cols: pl.BlockSpec((tm, cols), lambda i: (i, 0))
    out_shape = [jax.ShapeDtypeStruct((n, d), F32)]
    out_specs = [row(d)]
    if emit_v:
        out_shape.append(jax.ShapeDtypeStruct((n, bw), F32))
        out_specs.append(row(bw))
    res = pl.pallas_call(
        functools.partial(_gmlp_kernel, emit_v=emit_v),
        grid=(n // tm,),
        in_specs=[row(d), row(a_prev.shape[1]), _layer_spec(w_prev.shape, layer),
                  _const_spec((1, d)), _layer_spec(win.shape, layer), _const_spec((1, bw)),
                  _const_spec(mix.shape), _const_spec(bias.shape), _layer_spec(wout.shape, layer)],
        out_specs=tuple(out_specs),
        out_shape=tuple(out_shape),
        scratch_shapes=[pltpu.VMEM((tm, bw), BF16)],
        compiler_params=_params(1),
        name="gmlp_layer",
    )(x, a_prev, w_prev, g.reshape(1, d), win, vg.reshape(1, bw), mix, bias, wout)
    return res if emit_v else res[0]


def _sgu_mask():
    c = jnp.arange(MLP_CHUNK) // STREAM_CHUNK
    return c[None, :] <= c[:, None]


def kernel(x_prompt, x_sample, cache_k, cache_v, cache_logf, norm_g, w_in_a, b_f, q_g, k_g, w_out_a,
           w_in_b, v_g, ws, bs, w_out_b):
    bsz, s_len, d = x_prompt.shape
    dbsz, t_new, _ = x_sample.shape
    p_len = cache_k.shape[2]
    width = HEADS * HEAD_DIM
    depth = norm_g.shape[0]
    assert depth % 2 == 0
    assert ROW_TILE == ATTN_BLOCK and HEADS % HEADS_PER_STEP == 0
    assert s_len % ATTN_BLOCK == 0 and s_len % CUMSUM_BLOCK == 0 and p_len % CUMSUM_BLOCK == 0
    assert p_len % SAMPLE_KV_BLOCK == 0 and MLP_CHUNK == 2 * t_new and GMLP_ROW_TILE % MLP_CHUNK == 0

    xp = x_prompt.reshape(bsz * s_len, d)
    xs = x_sample.reshape(dbsz * t_new, d)
    n_a = w_in_a.shape[0]
    ck = cache_k.reshape(n_a, dbsz, p_len * HEADS, HEAD_DIM)
    cv = cache_v.reshape(n_a, dbsz, p_len * HEADS, HEAD_DIM)
    kv_p, kv_s = None, None
    lp_l, ls_l, sgu_l = [], [], []
    w = w_in_a.astype(BF16)
    wf = jnp.pad(w_in_a[:, :, 4 * width:], ((0, 0), (0, 0), (0, HEAD_DIM - HEADS))).astype(BF16)
    wo = w_out_a.astype(BF16)
    win = w_in_b.astype(BF16)
    wout = w_out_b.astype(BF16)
    bw = wout.shape[1]
    for i in range(depth):
        j = i // 2
        if i % 2 == 0:

            q16, k_all, v_all, k16, v16, sg, logf = _fox_inproj(xp, norm_g[i], w, wf, b_f[j], q_g[j], k_g[j],
                                                                j, n_a, kv_p, v_transposed=True)
            kv_p = (k_all, v_all)
            nb = s_len // CUMSUM_BLOCK
            lt = jnp.transpose(logf.reshape(bsz, s_len, HEADS), (0, 2, 1))
            fcum = _cumsum_lanes(lt.reshape(bsz, HEADS, nb, CUMSUM_BLOCK), chained=True)
            fcum = fcum.reshape(bsz, HEADS, s_len // ATTN_BLOCK, 1, ATTN_BLOCK)
            og_p = _prompt_attn(q16, k16, v16, sg, fcum, bsz, s_len)
            lp_l.append(logf.reshape(bsz, s_len, HEADS))

            q16, k_all, v_all, k16, v16, sg, logf = _fox_inproj(xs, norm_g[i], w, wf, b_f[j], q_g[j], k_g[j],
                                                                j, n_a, kv_s, v_transposed=False)
            kv_s = (k_all, v_all)
            nbp = p_len // CUMSUM_BLOCK
            cl = jnp.transpose(cache_logf[j], (0, 2, 1)).reshape(dbsz, HEADS, nbp, CUMSUM_BLOCK)
            f_past = _cumsum_lanes(cl, chained=True).reshape(dbsz, HEADS, p_len)
            lt = jnp.transpose(logf.reshape(dbsz, t_new, HEADS), (0, 2, 1))
            f_new = _cumsum_lanes(lt.reshape(1, HEADS, dbsz, t_new), chained=False)
            f_new = f_new.reshape(dbsz, HEADS, t_new)
            og_s = _sample_attn(q16, ck, cv, j, k16, v16, sg, f_past, f_past[:, :, p_len - 1:], f_new, dbsz, t_new)
            ls_l.append(logf.reshape(dbsz, t_new, HEADS))
        else:
            mix_p = (ws[j] * _sgu_mask()[None]).astype(BF16)
            bias_p = jnp.transpose(bs[j])
            xp = _gmlp_layer(xp, og_p, wo, norm_g[i], win, v_g[j], mix_p, bias_p, wout, j, emit_v=False)
            a = (ws[j] * _sgu_mask()[None])[:, :t_new, :t_new]
            z = jnp.zeros_like(a)
            mix_s = jnp.concatenate([jnp.concatenate([a, z], axis=2),
                                     jnp.concatenate([z, a], axis=2)], axis=1).astype(BF16)
            bias_s = jnp.transpose(jnp.concatenate([bs[j][:, :t_new], bs[j][:, :t_new]], axis=1))
            xs, sv = _gmlp_layer(xs, og_s, wo, norm_g[i], win, v_g[j], mix_s, bias_s, wout, j, emit_v=True)
            sgu_l.append(sv.reshape(dbsz, t_new, bw))
    kv5 = lambda a, nb_, t_: a.reshape(n_a, nb_, t_, HEADS, HEAD_DIM)
    return (xp.reshape(bsz, s_len, d), xs.reshape(dbsz, t_new, d),
            kv5(kv_p[0], bsz, s_len), kv5(kv_p[1], bsz, s_len), jnp.stack(lp_l),
            kv5(kv_s[0], dbsz, t_new), kv5(kv_s[1], dbsz, t_new), jnp.stack(ls_l), jnp.stack(sgu_l))
```

```python
import functools

import jax
import jax.numpy as jnp
from jax import lax
from jax.experimental import pallas as pl
from jax.experimental.pallas import tpu as pltpu

F32 = jnp.float32
BF16 = jnp.bfloat16

EPS = 1e-6
NEG_INF = -1e30
LOG2E = 1.4426950408889634

HEADS = 8
HEAD_DIM = 128
Q_PRESCALE = HEAD_DIM ** -0.5 * LOG2E
GROUPS = 8
MLP_CHUNK = 128
STREAM_CHUNK = 64

V7X_LANES = 128
V7X_VMEM_BYTES = 64 * 1024 * 1024
VMEM_LIMIT_BYTES = V7X_VMEM_BYTES - 8 * 1024 * 1024

ROW_TILE = 512
GMLP_ROW_TILE = 512
ATTN_BLOCK = 512
HEADS_PER_STEP = 4
SAMPLE_KV_BLOCK = 2048
SAMPLE_KV_SPLIT = 4
CUMSUM_BLOCK = 256
CUMSUM_ROWS = 512


def _params(n_axes):
    return pltpu.CompilerParams(dimension_semantics=("arbitrary",) * n_axes,
                                vmem_limit_bytes=VMEM_LIMIT_BYTES)


def _const_spec(shape):
    nd = len(shape)
    return pl.BlockSpec(shape, lambda *_: (0,) * nd, pipeline_mode=pl.Buffered(1))


def _layer_spec(stacked_shape, layer):
    nd = len(stacked_shape) - 1
    return pl.BlockSpec((None,) + tuple(stacked_shape[1:]), lambda *_: (layer,) + (0,) * nd,
                        pipeline_mode=pl.Buffered(1))


def _rmsnorm(x, g):
    ms = jnp.mean(x * x, axis=-1, keepdims=True)
    return x * lax.rsqrt(ms + EPS) * g


def _dot(a, b):
    return jnp.dot(a, b, preferred_element_type=F32)


def _dot_nt(a, b):
    return lax.dot_general(a, b, (((1,), (1,)), ((), ())), preferred_element_type=F32)


def _fox_inproj_kernel(x_ref, g_ref, w_ref, wf_ref, bf_ref, qg_ref, kg_ref, *rest, layer, first, v_transposed):
    q16_ref, k32_ref, v32_ref, k16_ref, v16_ref, sg_ref, logf_ref = rest[-7:]
    if first:
        for slot in range(k32_ref.shape[0]):
            if slot != layer:
                k32_ref[slot] = jnp.zeros(k32_ref.shape[1:], F32)
                v32_ref[slot] = jnp.zeros(v32_ref.shape[1:], F32)
        k32_ref = k32_ref.at[layer]
        v32_ref = v32_ref.at[layer]
    tm = x_ref.shape[0]
    width = HEADS * HEAD_DIM
    h = _rmsnorm(x_ref[...], g_ref[...]).astype(BF16)

    zq = _dot(h, w_ref[:, 0:width])
    for hd in range(HEADS):
        sl = slice(hd * HEAD_DIM, (hd + 1) * HEAD_DIM)
        q16_ref[:, sl] = (_rmsnorm(zq[:, sl], qg_ref[...]) * Q_PRESCALE).astype(BF16)

    zk = _dot(h, w_ref[:, width:2 * width])
    for hd in range(HEADS):
        sl = slice(hd * HEAD_DIM, (hd + 1) * HEAD_DIM)
        kn = _rmsnorm(zk[:, sl], kg_ref[...])
        k32_ref[pl.ds(hd, tm, stride=HEADS), :] = kn
        k16_ref[:, sl] = kn.astype(BF16)

    zv = _dot(h, w_ref[:, 2 * width:3 * width])
    for hd in range(HEADS):
        sl = slice(hd * HEAD_DIM, (hd + 1) * HEAD_DIM)
        v32_ref[pl.ds(hd, tm, stride=HEADS), :] = zv[:, sl]
        if v_transposed:
            v16_ref[sl, :] = jnp.transpose(zv[:, sl]).astype(BF16)
    if not v_transposed:
        v16_ref[...] = zv.astype(BF16)

    zg = _dot(h, w_ref[:, 3 * width:4 * width])
    sg_ref[...] = jax.nn.silu(zg).astype(BF16)

    zf = _dot(h, wf_ref[...])
    logf_ref[...] = jax.nn.log_sigmoid(zf[:, 0:HEADS] + bf_ref[...])


def _fox_inproj(x, g, w, wf, b_f, q_g, k_g, layer, n_layers, kv_prev, v_transposed):
    n, d = x.shape
    width = HEADS * HEAD_DIM
    tm = ROW_TILE
    first = kv_prev is None
    row = lambda cols: pl.BlockSpec((tm, cols), lambda i: (i, 0))
    if first:
        kv_spec = pl.BlockSpec((n_layers, tm * HEADS, HEAD_DIM), lambda i: (0, i, 0))
    else:
        kv_spec = pl.BlockSpec((None, tm * HEADS, HEAD_DIM), lambda i: (layer, i, 0))
    kv_shape = jax.ShapeDtypeStruct((n_layers, n * HEADS, HEAD_DIM), F32)
    if v_transposed:
        v16_shape = jax.ShapeDtypeStruct((n // tm, width, tm), BF16)
        v16_spec = pl.BlockSpec((None, width, tm), lambda i: (i, 0, 0))
    else:
        v16_shape = jax.ShapeDtypeStruct((n, width), BF16)
        v16_spec = row(width)
    out_shape = (
        jax.ShapeDtypeStruct((n, width), BF16),
        kv_shape,
        kv_shape,
        jax.ShapeDtypeStruct((n, width), BF16),
        v16_shape,
        jax.ShapeDtypeStruct((n, width), BF16),
        jax.ShapeDtypeStruct((n, HEADS), F32),
    )
    in_specs = [row(d), _const_spec((1, d)), _layer_spec(w.shape, layer), _layer_spec(wf.shape, layer),
                _const_spec((1, HEADS)), _const_spec((1, HEAD_DIM)), _const_spec((1, HEAD_DIM))]
    args = [x, g.reshape(1, d), w, wf, b_f.reshape(1, HEADS), q_g.reshape(1, HEAD_DIM), k_g.reshape(1, HEAD_DIM)]
    aliases = {}
    if not first:
        in_specs += [pl.BlockSpec(memory_space=pl.ANY)] * 2
        aliases = {len(args): 1, len(args) + 1: 2}
        args += list(kv_prev)
    return pl.pallas_call(
        functools.partial(_fox_inproj_kernel, layer=layer, first=first, v_transposed=v_transposed),
        grid=(n // tm,),
        in_specs=in_specs,
        out_specs=(row(width), kv_spec, kv_spec, row(width), v16_spec, row(width), row(HEADS)),
        out_shape=out_shape,
        input_output_aliases=aliases,
        compiler_params=_params(1),
        name="fox_inproj",
    )(*args)


def _split3(x):
    hi = x.astype(BF16)
    r1 = x - hi.astype(F32)
    mid = r1.astype(BF16)
    lo = (r1 - mid.astype(F32)).astype(BF16)
    return hi, mid, lo


def _cumsum_kernel(x_ref, o_ref, *, chain_shift):
    rows, lb = x_ref.shape
    r = lax.broadcasted_iota(jnp.int32, (lb, lb), 0)
    c = lax.broadcasted_iota(jnp.int32, (lb, lb), 1)
    upper = (r <= c).astype(BF16)
    hi, mid, lo = _split3(x_ref[...])
    cs = _dot(hi, upper) + _dot(mid, upper) + _dot(lo, upper)
    if chain_shift:
        rb = lax.broadcasted_iota(jnp.int32, (rows, rows), 0)
        cb = lax.broadcasted_iota(jnp.int32, (rows, rows), 1)
        tot = jnp.broadcast_to(cs[:, lb - 1:lb], (rows, rows))
        tot_row = jnp.sum(jnp.where(rb == cb, tot, 0.0), axis=0, keepdims=True)
        earlier = (cb < rb) & (lax.shift_right_logical(cb, chain_shift) == lax.shift_right_logical(rb, chain_shift))
        offs = jnp.sum(jnp.where(earlier, jnp.broadcast_to(tot_row, (rows, rows)), 0.0),
                       axis=1, keepdims=True)
        cs = cs + offs
    o_ref[...] = cs


def _cumsum_lanes(x, chained):
    bsz, nh, nb, lb = x.shape
    chain_shift = (nb.bit_length() - 1) if chained else 0
    assert nb == 1 << (nb.bit_length() - 1)
    total = bsz * nh * nb
    rows = min(total, CUMSUM_ROWS)
    assert total % rows == 0 and rows % nb == 0
    spec = pl.BlockSpec((None, rows, lb), lambda b: (b, 0, 0))
    out = pl.pallas_call(
        functools.partial(_cumsum_kernel, chain_shift=chain_shift),
        grid=(total // rows,),
        in_specs=[spec],
        out_specs=spec,
        out_shape=jax.ShapeDtypeStruct((total // rows, rows, lb), F32),
        compiler_params=_params(1),
        name="logf_cumsum",
    )(x.reshape(total // rows, rows, lb))
    return out.reshape(x.shape)


def _row_to_col(row):
    n = row.shape[1]
    c = min(n, V7X_LANES)
    eye =lax.broadcasted_iota(jnp.int32, (c, c), 0) == lax.broadcasted_iota(jnp.int32, (c, c), 1)
    pieces = [jnp.sum(jnp.where(eye, jnp.broadcast_to(row[:, i:i + c], (c, c)), 0.0), axis=1, keepdims=True)
              for i in range(0, n, c)]
    return pieces[0] if len(pieces) == 1 else jnp.concatenate(pieces, axis=0)


def _prompt_attn_kernel(q_ref, k_ref, vt_ref, sg_ref, f_ref, og_ref, m_s, l_s, acc_s, fk_s):
    t = ATTN_BLOCK
    hp, nblk = f_ref.shape[0], f_ref.shape[1]
    heads = range(hp)
    hs = lambda h: slice(h * HEAD_DIM, (h + 1) * HEAD_DIM)
    key_i = lax.broadcasted_iota(jnp.int32, (t, t), 0)
    qry_i = lax.broadcasted_iota(jnp.int32, (t, t), 1)
    causal = key_i <= qry_i
    for h in heads:
        for blk in range(nblk):
            fk_s[h, blk] = _row_to_col(f_ref[h, blk] * LOG2E)

    def step(q, fq, k_of, vt_of, fk_of, masked):
        ut = [_dot_nt(k_of(h), q[h]) - fk_of(h) for h in heads]
        if masked:
            ut = [jnp.where(causal, x, NEG_INF) for x in ut]
        m_old = [m_s[h] for h in heads]
        m_new = [jnp.maximum(m_old[h], jnp.max(ut[h], axis=0, keepdims=True) + fq[h]) for h in heads]
        pt = [jnp.exp2(ut[h] - (m_new[h] - fq[h])) for h in heads]
        alpha = [jnp.exp2(m_old[h] - m_new[h]) for h in heads]
        for h in heads:
            m_s[h] = m_new[h]
            l_s[h] = alpha[h] * l_s[h] + jnp.sum(pt[h], axis=0, keepdims=True)
        for h in heads:
            acc_s[h] = alpha[h] * acc_s[h] + _dot(vt_of(h), pt[h].astype(BF16))

    for qi in range(nblk):
        qs = slice(qi * t, (qi + 1) * t)
        q = [q_ref[qs, hs(h)] for h in heads]
        fq = [f_ref[h, qi] * LOG2E for h in heads]
        m_s[...] = jnp.full(m_s.shape, NEG_INF, F32)
        l_s[...] = jnp.zeros(l_s.shape, F32)
        acc_s[...] = jnp.zeros(acc_s.shape, F32)

        def body(ki, carry, q=q, fq=fq):
            start = pl.multiple_of(ki * t, t)
            step(q, fq, lambda h: k_ref[pl.ds(start, t), hs(h)], lambda h: vt_ref[ki, hs(h), :],
                 lambda h: fk_s[h, ki], False)
            return carry

        lax.fori_loop(0, qi, body, 0)
        step(q, fq, lambda h: k_ref[qs, hs(h)], lambda h: vt_ref[qi, hs(h), :], lambda h: fk_s[h, qi], True)
        for h in heads:
            o = jnp.transpose(acc_s[h] / l_s[h])
            og_ref[qs, hs(h)] = (o * sg_ref[qs, hs(h)]).astype(BF16)


def _prompt_attn(q16, k16, vt16, sg, fcum, bsz, s_len):
    t = ATTN_BLOCK
    hp = HEADS_PER_STEP
    nblk = s_len // t
    n, width = q16.shape
    blk = pl.BlockSpec((s_len, hp * HEAD_DIM), lambda b, g: (b, g))
    return pl.pallas_call(
        _prompt_attn_kernel,
        grid=(bsz, HEADS // hp),
        in_specs=[blk, blk,
                  pl.BlockSpec((nblk, hp * HEAD_DIM, t), lambda b, g: (b, g, 0)),
                  blk,
                  pl.BlockSpec((None, hp, nblk, 1, t), lambda b, g: (b, g, 0, 0, 0))],
        out_specs=blk,
        out_shape=jax.ShapeDtypeStruct((n, width), BF16),
        scratch_shapes=[pltpu.VMEM((hp, 1, t), F32), pltpu.VMEM((hp, 1, t), F32),
                        pltpu.VMEM((hp, HEAD_DIM, t), F32), pltpu.VMEM((hp, nblk, t, 1), F32)],
        compiler_params=_params(2),
        name="prompt_attn",
    )(q16, k16, vt16, sg, fcum)


def _sample_attn_kernel(q_ref, kn_ref, vn_ref, sg_ref, fp_ref, fl_ref, fn_ref, *rest, n_split):
    ck_refs, cv_refs = rest[:n_split], rest[n_split:2 * n_split]
    og_ref, m_ref, l_ref, acc_ref, fq_ref = rest[2 * n_split:]

    def head_rows(refs, hd):
        rows_per_piece = refs[0].shape[0] // HEADS
        pieces = [r[pl.ds(hd, rows_per_piece, stride=HEADS), :] for r in refs]
        return (pieces[0] if n_split == 1 else jnp.concatenate(pieces, axis=0)).astype(BF16)

    ki = pl.program_id(1)
    last = pl.num_programs(1) - 1
    t = q_ref.shape[0]
    heads = range(HEADS)
    hs = lambda hd: slice(hd * HEAD_DIM, (hd + 1) * HEAD_DIM)
    rows = lax.broadcasted_iota(jnp.int32, (t, t), 0)
    cols = lax.broadcasted_iota(jnp.int32, (t, t), 1)

    @pl.when(ki == 0)
    def _():
        m_ref[...] = jnp.full(m_ref.shape, NEG_INF, F32)
        l_ref[...] = jnp.zeros(l_ref.shape, F32)
        acc_ref[...] = jnp.zeros(acc_ref.shape, F32)
        f_new = (fl_ref[...] + fn_ref[...]) * LOG2E
        for hd in heads:
            fq_ref[hd] = _row_to_col(f_new[hd:hd + 1, :])

    def update(with_new):
        f_new = (fl_ref[...] + fn_ref[...]) * LOG2E
        f_past = fp_ref[...] * LOG2E
        s = []
        for hd in heads:
            q = q_ref[:, hs(hd)]
            k = head_rows(ck_refs, hd)
            segs =[_dot_nt(q, k) - f_past[hd:hd + 1, :]]
            if with_new:
                sn = _dot_nt(q, kn_ref[:, hs(hd)]) - f_new[hd:hd + 1, :]
                segs.append(jnp.where(cols <= rows, sn, NEG_INF))
            s.append(segs)
        fq = [fq_ref[hd] for hd in heads]
        m_old = [m_ref[hd] for hd in heads]
        m_new = [jnp.maximum(m_old[hd], functools.reduce(
            jnp.maximum, [jnp.max(x, axis=1, keepdims=True) for x in s[hd]]) + fq[hd]) for hd in heads]
        p = [[jnp.exp2(x - (m_new[hd] - fq[hd])) for x in s[hd]] for hd in heads]
        alpha = [jnp.exp2(m_old[hd] - m_new[hd]) for hd in heads]
        l_new = [alpha[hd] * l_ref[hd] + sum(jnp.sum(x, axis=1, keepdims=True) for x in p[hd]) for hd in heads]
        for hd in heads:
            v = head_rows(cv_refs, hd)
            pv =_dot(p[hd][0].astype(BF16), v)
            if with_new:
                pv = pv + _dot(p[hd][1].astype(BF16), vn_ref[:, hs(hd)])
            acc = alpha[hd] * acc_ref[hd] + pv
            if with_new:
                og_ref[:, hs(hd)] = (acc / l_new[hd] * sg_ref[:, hs(hd)]).astype(BF16)
            else:
                m_ref[hd] = m_new[hd]
                l_ref[hd] = l_new[hd]
                acc_ref[hd] = acc

    @pl.when(ki != last)
    def _():
        update(False)

    @pl.when(ki == last)
    def _():
        update(True)


def _sample_attn(q16, cache_k, cache_v, layer, k16, v16, sg, f_past, f_last, f_new, bsz, t):
    n, width = q16.shape
    p_len = f_past.shape[2]
    tk = SAMPLE_KV_BLOCK
    ns = SAMPLE_KV_SPLIT
    new = pl.BlockSpec((t, width), lambda b, k: (b, 0))
    past = [pl.BlockSpec((None, None, tk // ns * HEADS, HEAD_DIM), lambda b, k, c=c: (layer, b, k * ns + c, 0))
            for c in range(ns)]
    return pl.pallas_call(
        functools.partial(_sample_attn_kernel, n_split=ns),
        grid=(bsz, p_len // tk),
        in_specs=[new, new, new, new,
                  pl.BlockSpec((None, HEADS, tk), lambda b, k: (b, 0, k)),
                  pl.BlockSpec((None, HEADS, 1), lambda b, k: (b, 0, 0)),
                  pl.BlockSpec((None, HEADS, t), lambda b, k: (b, 0, 0))] + past + past,
        out_specs=new,
        out_shape=jax.ShapeDtypeStruct((n, width), BF16),
        scratch_shapes=[pltpu.VMEM((HEADS, t, 1), F32), pltpu.VMEM((HEADS, t, 1), F32),
                        pltpu.VMEM((HEADS, t, HEAD_DIM), F32), pltpu.VMEM((HEADS, t, 1), F32)],
        compiler_params=_params(2),
        name="sample_attn",
    )(q16, k16, v16, sg, f_past, f_last, f_new, *([cache_k] * ns), *([cache_v] * ns))


def _gmlp_kernel(x_ref, ap_ref, wp_ref, g_ref, win_ref, vg_ref, mix_ref, bias_ref, wout_ref, *rest, emit_v):
    if emit_v:
        o_ref, vn_ref, a_ref = rest
    else:
        o_ref, a_ref = rest
    tm = x_ref.shape[0]
    bw = wout_ref.shape[0]
    gd = bw // GROUPS
    x = x_ref[...] + _dot(ap_ref[...], wp_ref[...])
    h = _rmsnorm(x, g_ref[...]).astype(BF16)
    v = _dot(h, win_ref[:, bw:bw + gd])
    for gi in range(GROUPS):
        sl = slice(gi * gd, (gi + 1) * gd)
        u = _dot(h, win_ref[:, sl])
        gate = _dot(h, win_ref[:, 2 * bw + gi * gd:2 * bw + (gi + 1) * gd])
        vn = _rmsnorm(v, vg_ref[:, sl])
        if gi + 1 < GROUPS:
            v = _dot(h, win_ref[:, bw + (gi + 1) * gd:bw + (gi + 2) * gd])
        if emit_v:
            vn_ref[:, sl] = vn
        vn16 = vn.astype(BF16)
        mixed = jnp.concatenate(
            [_dot(mix_ref[gi], vn16[ci * MLP_CHUNK:(ci + 1) * MLP_CHUNK, :]) + bias_ref[:, gi:gi + 1]
             for ci in range(tm // MLP_CHUNK)], axis=0)
        a_ref[:, sl] = (u * mixed * jax.nn.silu(gate)).astype(BF16)
    o_ref[...] = x + _dot(a_ref[...], wout_ref[...])


def _gmlp_layer(x, a_prev, w_prev, g, win, vg, mix, bias, wout, layer, emit_v):
    n, d = x.shape
    bw = wout.shape[1]
    tm = GMLP_ROW_TILE
    row = lambda cols: pl.BlockSpec((tm, cols), lambda i: (i, 0))
    out_shape = [jax.ShapeDtypeStruct((n, d), F32)]
    out_specs = [row(d)]
    if emit_v:
        out_shape.append(jax.ShapeDtypeStruct((n, bw), F32))
        out_specs.append(row(bw))
    res = pl.pallas_call(
        functools.partial(_gmlp_kernel, emit_v=emit_v),
        grid=(n // tm,),
        in_specs=[row(d), row(a_prev.shape[1]), _layer_spec(w_prev.shape, layer),
                  _const_spec((1, d)), _layer_spec(win.shape, layer), _const_spec((1, bw)),
                  _const_spec(mix.shape), _const_spec(bias.shape), _layer_spec(wout.shape, layer)],
        out_specs=tuple(out_specs),
        out_shape=tuple(out_shape),
        scratch_shapes=[pltpu.VMEM((tm, bw), BF16)],
        compiler_params=_params(1),
        name="gmlp_layer",
    )(x, a_prev, w_prev, g.reshape(1, d), win, vg.reshape(1, bw), mix, bias, wout)
    return res if emit_v else res[0]


def _sgu_mask():
    c = jnp.arange(MLP_CHUNK) // STREAM_CHUNK
    return c[None, :] <= c[:, None]


def kernel(x_prompt, x_sample, cache_k, cache_v, cache_logf, norm_g, w_in_a, b_f, q_g, k_g, w_out_a,
           w_in_b, v_g, ws, bs, w_out_b):
    bsz, s_len, d = x_prompt.shape
    dbsz, t_new, _ = x_sample.shape
    p_len = cache_k.shape[2]
    width = HEADS * HEAD_DIM
    depth = norm_g.shape[0]
    assert depth % 2 == 0
    assert ROW_TILE == ATTN_BLOCK and HEADS % HEADS_PER_STEP == 0
    assert s_len % ATTN_BLOCK == 0 and s_len % CUMSUM_BLOCK == 0 and p_len % CUMSUM_BLOCK == 0
    assert p_len % SAMPLE_KV_BLOCK == 0 and MLP_CHUNK == 2 * t_new and GMLP_ROW_TILE % MLP_CHUNK == 0

    xp = x_prompt.reshape(bsz * s_len, d)
    xs = x_sample.reshape(dbsz * t_new, d)
    n_a = w_in_a.shape[0]
    ck = cache_k.reshape(n_a, dbsz, p_len * HEADS, HEAD_DIM)
    cv = cache_v.reshape(n_a, dbsz, p_len * HEADS, HEAD_DIM)
    kv_p, kv_s = None, None
    lp_l, ls_l, sgu_l = [], [], []
    w = w_in_a[:, :, :4 * width].astype(BF16)
    wf =jnp.pad(w_in_a[:, :, 4 * width:], ((0, 0), (0, 0), (0, HEAD_DIM - HEADS))).astype(BF16)
    wo = w_out_a.astype(BF16)
    win = w_in_b.astype(BF16)
    wout = w_out_b.astype(BF16)
    bw = wout.shape[1]
    for i in range(depth):
        j = i // 2
        if i % 2 == 0:

            q16, k_all, v_all, k16, v16, sg, logf = _fox_inproj(xp, norm_g[i], w, wf, b_f[j], q_g[j], k_g[j],
                                                                j, n_a, kv_p, v_transposed=True)
            kv_p = (k_all, v_all)
            nb = s_len // CUMSUM_BLOCK
            lt = jnp.transpose(logf.reshape(bsz, s_len, HEADS), (0, 2, 1))
            fcum = _cumsum_lanes(lt.reshape(bsz, HEADS, nb, CUMSUM_BLOCK), chained=True)
            fcum = fcum.reshape(bsz, HEADS, s_len // ATTN_BLOCK, 1, ATTN_BLOCK)
            og_p = _prompt_attn(q16, k16, v16, sg, fcum, bsz, s_len)
            lp_l.append(logf.reshape(bsz, s_len, HEADS))

            q16, k_all, v_all, k16, v16, sg, logf = _fox_inproj(xs, norm_g[i], w, wf, b_f[j], q_g[j], k_g[j],
                                                                j, n_a, kv_s, v_transposed=False)
            kv_s = (k_all, v_all)
            nbp = p_len // CUMSUM_BLOCK
            cl = jnp.transpose(cache_logf[j], (0, 2, 1)).reshape(dbsz, HEADS, nbp, CUMSUM_BLOCK)
            f_past = _cumsum_lanes(cl, chained=True).reshape(dbsz, HEADS, p_len)
            lt = jnp.transpose(logf.reshape(dbsz, t_new, HEADS), (0, 2, 1))
            f_new = _cumsum_lanes(lt.reshape(1, HEADS, dbsz, t_new), chained=False)
            f_new = f_new.reshape(dbsz, HEADS, t_new)
            og_s = _sample_attn(q16, ck, cv, j, k16, v16, sg, f_past, f_past[:, :, p_len - 1:], f_new, dbsz, t_new)
            ls_l.append(logf.reshape(dbsz, t_new, HEADS))
        else:
            mix_p = (ws[j] * _sgu_mask()[None]).astype(BF16)
            bias_p = jnp.transpose(bs[j])
            xp = _gmlp_layer(xp, og_p, wo, norm_g[i], win, v_g[j], mix_p, bias_p, wout, j, emit_v=False)
            a = (ws[j] * _sgu_mask()[None])[:, :t_new, :t_new]
            z = jnp.zeros_like(a)
            mix_s = jnp.concatenate([jnp.concatenate([a, z], axis=2),
                                     jnp.concatenate([z, a], axis=2)], axis=1).astype(BF16)
            bias_s = jnp.transpose(jnp.concatenate([bs[j][:, :t_new], bs[j][:, :t_new]], axis=1))
            xs, sv = _gmlp_layer(xs, og_s, wo, norm_g[i], win, v_g[j], mix_s, bias_s, wout, j, emit_v=True)
            sgu_l.append(sv.reshape(dbsz, t_new, bw))
    kv5 = lambda a, nb_, t_: a.reshape(n_a, nb_, t_, HEADS, HEAD_DIM)
    return (xp.reshape(bsz, s_len, d), xs.reshape(dbsz, t_new, d),
            kv5(kv_p[0], bsz, s_len), kv5(kv_p[1], bsz, s_len), jnp.stack(lp_l),
            kv5(kv_s[0], dbsz, t_new), kv5(kv_s[1], dbsz, t_new), jnp.stack(ls_l), jnp.stack(sgu_l))
```

```python
import functools

import jax
import jax.numpy as jnp
from jax import lax
from jax.experimental import pallas as pl
from jax.experimental.pallas import tpu as pltpu

F32 = jnp.float32
BF16 = jnp.bfloat16

EPS = 1e-6
NEG_INF = -1e30
LOG2E = 1.4426950408889634

HEADS = 8
HEAD_DIM = 128
Q_PRESCALE = HEAD_DIM ** -0.5 * LOG2E
GROUPS = 8
MLP_CHUNK = 128
STREAM_CHUNK = 64

V7X_LANES = 128
BF16_SUBLANE_ROWS = 16
V7X_VMEM_BYTES = 64 * 1024 * 1024
VMEM_LIMIT_BYTES = V7X_VMEM_BYTES - 8 * 1024 * 1024

ROW_TILE = 512
GMLP_ROW_TILE = 512
ATTN_BLOCK = 512
HEADS_PER_STEP = 4
SAMPLE_KV_BLOCK = 2048
CUMSUM_BLOCK = 256
CUMSUM_ROWS = 512


def _params(n_axes):
    return pltpu.CompilerParams(dimension_semantics=("arbitrary",) * n_axes,
                                vmem_limit_bytes=VMEM_LIMIT_BYTES)


def _const_spec(shape):
    nd = len(shape)
    return pl.BlockSpec(shape, lambda *_: (0,) * nd, pipeline_mode=pl.Buffered(1))


def _layer_spec(stacked_shape, layer):
    nd = len(stacked_shape) - 1
    return pl.BlockSpec((None,) + tuple(stacked_shape[1:]), lambda *_: (layer,) + (0,) * nd,
                        pipeline_mode=pl.Buffered(1))


def _rmsnorm(x, g):
    ms = jnp.mean(x * x, axis=-1, keepdims=True)
    return x * lax.rsqrt(ms + EPS) * g


def _dot(a, b):
    return jnp.dot(a, b, preferred_element_type=F32)


def _dot_nt(a, b):
    return lax.dot_general(a, b, (((1,), (1,)), ((), ())), preferred_element_type=F32)


def _fox_inproj_kernel(x_ref, g_ref, w_ref, wf_ref, bf_ref, qg_ref, kg_ref, *rest, layer, first, v_transposed):
    q16_ref, k32_ref, v32_ref, k16_ref, v16_ref, sg_ref, logf_ref = rest[-7:]
    if first:
        for slot in range(k32_ref.shape[0]):
            if slot != layer:
                k32_ref[slot] = jnp.zeros(k32_ref.shape[1:], F32)
                v32_ref[slot] = jnp.zeros(v32_ref.shape[1:], F32)
        k32_ref = k32_ref.at[layer]
        v32_ref = v32_ref.at[layer]
    tm = x_ref.shape[0]
    width = HEADS * HEAD_DIM
    h = _rmsnorm(x_ref[...], g_ref[...]).astype(BF16)

    zq = _dot(h, w_ref[:, 0:width])
    for hd in range(HEADS):
        sl = slice(hd * HEAD_DIM, (hd + 1) * HEAD_DIM)
        q16_ref[:, sl] = (_rmsnorm(zq[:, sl], qg_ref[...]) * Q_PRESCALE).astype(BF16)

    zk = _dot(h, w_ref[:, width:2 * width])
    for hd in range(HEADS):
        sl = slice(hd * HEAD_DIM, (hd + 1) * HEAD_DIM)
        kn = _rmsnorm(zk[:, sl], kg_ref[...])
        k32_ref[pl.ds(hd, tm, stride=HEADS), :] = kn
        k16_ref[:, sl] = kn.astype(BF16)

    zv = _dot(h, w_ref[:, 2 * width:3 * width])
    for hd in range(HEADS):
        sl = slice(hd * HEAD_DIM, (hd + 1) * HEAD_DIM)
        v32_ref[pl.ds(hd, tm, stride=HEADS), :] = zv[:, sl]
        if v_transposed:
            v16_ref[sl, :] = jnp.transpose(zv[:, sl]).astype(BF16)
    if not v_transposed:
        v16_ref[...] = zv.astype(BF16)

    zg = _dot(h, w_ref[:, 3 * width:4 * width])
    sg_ref[...] = jax.nn.silu(zg)

    zf = _dot(h, wf_ref[...])
    logf_ref[...] = jax.nn.log_sigmoid(zf[:, 0:HEADS] + bf_ref[...])


def _fox_inproj(x, g, w, wf, b_f, q_g, k_g, layer, n_layers, kv_prev, v_transposed):
    n, d = x.shape
    width = HEADS * HEAD_DIM
    tm = ROW_TILE
    first = kv_prev is None
    row = lambda cols: pl.BlockSpec((tm, cols), lambda i: (i, 0))
    if first:
        kv_spec = pl.BlockSpec((n_layers, tm * HEADS, HEAD_DIM), lambda i: (0, i, 0))
    else:
        kv_spec = pl.BlockSpec((None, tm * HEADS, HEAD_DIM), lambda i: (layer, i, 0))
    kv_shape = jax.ShapeDtypeStruct((n_layers, n * HEADS, HEAD_DIM), F32)
    if v_transposed:
        v16_shape = jax.ShapeDtypeStruct((n // tm, width, tm), BF16)
        v16_spec = pl.BlockSpec((None, width, tm), lambda i: (i, 0, 0))
    else:
        v16_shape = jax.ShapeDtypeStruct((n, width), BF16)
        v16_spec = row(width)
    out_shape = (
        jax.ShapeDtypeStruct((n, width), BF16),
        kv_shape,
        kv_shape,
        jax.ShapeDtypeStruct((n, width), BF16),
        v16_shape,
        jax.ShapeDtypeStruct((n, width), F32),
        jax.ShapeDtypeStruct((n, HEADS), F32),
    )
    in_specs = [row(d), _const_spec((1, d)), _layer_spec(w.shape, layer), _layer_spec(wf.shape, layer),
                _const_spec((1, HEADS)), _const_spec((1, HEAD_DIM)), _const_spec((1, HEAD_DIM))]
    args = [x, g.reshape(1, d), w, wf, b_f.reshape(1, HEADS), q_g.reshape(1, HEAD_DIM), k_g.reshape(1, HEAD_DIM)]
    aliases = {}
    if not first:
        in_specs += [pl.BlockSpec(memory_space=pl.ANY)] * 2
        aliases = {len(args): 1, len(args) + 1: 2}
        args += list(kv_prev)
    return pl.pallas_call(
        functools.partial(_fox_inproj_kernel, layer=layer, first=first, v_transposed=v_transposed),
        grid=(n // tm,),
        in_specs=in_specs,
        out_specs=(row(width), kv_spec, kv_spec, row(width), v16_spec, row(width), row(HEADS)),
        out_shape=out_shape,
        input_output_aliases=aliases,
        compiler_params=_params(1),
        name="fox_inproj",
    )(*args)


def _split3(x):
    hi = x.astype(BF16)
    r1 = x - hi.astype(F32)
    mid = r1.astype(BF16)
    lo = (r1 - mid.astype(F32)).astype(BF16)
    return hi, mid, lo


def _cumsum_kernel(x_ref, o_ref, *, chain_shift):
    rows, lb = x_ref.shape
    r = lax.broadcasted_iota(jnp.int32, (lb, lb), 0)
    c = lax.broadcasted_iota(jnp.int32, (lb, lb), 1)
    upper = (r <= c).astype(BF16)
    hi, mid, lo = _split3(x_ref[...])
    cs = _dot(hi, upper) + _dot(mid, upper) + _dot(lo, upper)
    if chain_shift:
        rb = lax.broadcasted_iota(jnp.int32, (rows, rows), 0)
        cb = lax.broadcasted_iota(jnp.int32, (rows, rows), 1)
        tot = jnp.broadcast_to(cs[:, lb - 1:lb], (rows, rows))
        tot_row = jnp.sum(jnp.where(rb == cb, tot, 0.0), axis=0, keepdims=True)
        earlier = (cb < rb) & (lax.shift_right_logical(cb, chain_shift) == lax.shift_right_logical(rb, chain_shift))
        offs = jnp.sum(jnp.where(earlier, jnp.broadcast_to(tot_row, (rows, rows)), 0.0),
                       axis=1, keepdims=True)
        cs = cs + offs
    o_ref[...] = cs


def _cumsum_lanes(x, chained):
    bsz, nh, nb, lb = x.shape
    chain_shift = (nb.bit_length() - 1) if chained else 0
    assert nb == 1 << (nb.bit_length() - 1)
    total = bsz * nh * nb
    rows = min(total, CUMSUM_ROWS)
    assert total % rows == 0 and rows % nb == 0
    spec = pl.BlockSpec((None, rows, lb), lambda b: (b, 0, 0))
    out = pl.pallas_call(
        functools.partial(_cumsum_kernel, chain_shift=chain_shift),
        grid=(total // rows,),
        in_specs=[spec],
        out_specs=spec,
        out_shape=jax.ShapeDtypeStruct((total // rows, rows, lb), F32),
        compiler_params=_params(1),
        name="logf_cumsum",
    )(x.reshape(total // rows, rows, lb))
    return out.reshape(x.shape)


def _row_to_col(row):
    n = row.shape[1]
    c = min(n, V7X_LANES)
    eye = lax.broadcasted_iota(jnp.int32, (c, c), 0) == lax.broadcasted_iota(jnp.int32, (c, c), 1)
    pieces = [jnp.sum(jnp.where(eye, jnp.broadcast_to(row[:, i:i + c], (c, c)), 0.0), axis=1, keepdims=True)
              for i in range(0, n, c)]
    return pieces[0] if len(pieces) == 1 else jnp.concatenate(pieces, axis=0)


def _prompt_attn_kernel(q_ref, k_ref, vt_ref, sg_ref, f_ref, og_ref, m_s, l_s, acc_s, fk_s):
    t = ATTN_BLOCK
    hp, nblk = f_ref.shape[0], f_ref.shape[1]
    heads = range(hp)
    hs = lambda h: slice(h * HEAD_DIM, (h + 1) * HEAD_DIM)
    key_i = lax.broadcasted_iota(jnp.int32, (t, t), 0)
    qry_i = lax.broadcasted_iota(jnp.int32, (t, t), 1)
    causal = key_i <= qry_i
    ones_rows = jnp.ones((BF16_SUBLANE_ROWS, t), BF16)
    for h in heads:
        for blk in range(nblk):
            fk_s[h, blk] = _row_to_col(f_ref[h, blk] * LOG2E)

    def step(q, fq, k_of, vt_of, fk_of, masked):
        ut = [_dot_nt(k_of(h), q[h]) - fk_of(h) for h in heads]
        if masked:
            ut = [jnp.where(causal, x, NEG_INF) for x in ut]
        m_old = [m_s[h] for h in heads]
        m_new = [jnp.maximum(m_old[h], jnp.max(ut[h], axis=0, keepdims=True) + fq[h]) for h in heads]
        pt = [jnp.exp2(ut[h] - (m_new[h] - fq[h])) for h in heads]
        alpha = [jnp.exp2(m_old[h] - m_new[h]) for h in heads]
        for h in heads:
            pv = _dot(jnp.concatenate([vt_of(h), ones_rows], axis=0), pt[h].astype(BF16))
            m_s[h] = m_new[h]
            l_s[h] = alpha[h] * l_s[h] + pv[HEAD_DIM:HEAD_DIM + 1, :]
            acc_s[h] = alpha[h] * acc_s[h] + pv[:HEAD_DIM, :]

    for qi in range(nblk):
        qs = slice(qi * t, (qi + 1) * t)
        q = [q_ref[qs, hs(h)] for h in heads]
        fq = [f_ref[h, qi] * LOG2E for h in heads]
        m_s[...] = jnp.full(m_s.shape, NEG_INF, F32)
        l_s[...] = jnp.zeros(l_s.shape, F32)
        acc_s[...] = jnp.zeros(acc_s.shape, F32)

        def body(ki, carry, q=q, fq=fq):
            start = pl.multiple_of(ki * t, t)
            step(q, fq, lambda h: k_ref[pl.ds(start, t), hs(h)], lambda h: vt_ref[ki, hs(h), :],
                 lambda h: fk_s[h, ki], False)
            return carry

        lax.fori_loop(0, qi, body, 0)
        step(q, fq, lambda h: k_ref[qs, hs(h)], lambda h: vt_ref[qi, hs(h), :], lambda h: fk_s[h, qi], True)
        for h in heads:
            o = jnp.transpose(acc_s[h] / l_s[h])
            og_ref[qs, hs(h)] = (o * sg_ref[qs, hs(h)]).astype(BF16)


def _prompt_attn(q16, k16, vt16, sg, fcum, bsz, s_len):
    t = ATTN_BLOCK
    hp = HEADS_PER_STEP
    nblk = s_len // t
    n, width = q16.shape
    blk = pl.BlockSpec((s_len, hp * HEAD_DIM), lambda b, g: (b, g))
    return pl.pallas_call(
        _prompt_attn_kernel,
        grid=(bsz, HEADS // hp),
        in_specs=[blk, blk,
                  pl.BlockSpec((nblk, hp * HEAD_DIM, t), lambda b, g: (b, g, 0)),
                  blk,
                  pl.BlockSpec((None, hp, nblk, 1, t), lambda b, g: (b, g, 0, 0, 0))],
        out_specs=blk,
        out_shape=jax.ShapeDtypeStruct((n, width), BF16),
        scratch_shapes=[pltpu.VMEM((hp, 1, t), F32), pltpu.VMEM((hp, 1, t), F32),
                        pltpu.VMEM((hp, HEAD_DIM, t), F32), pltpu.VMEM((hp, nblk, t, 1), F32)],
        compiler_params=_params(2),
        name="prompt_attn",
    )(q16, k16, vt16, sg, fcum)


def _sample_attn_kernel(q_ref, ck_ref, cv_ref, kn_ref, vn_ref, sg_ref, fp_ref, fl_ref, fn_ref,
                        og_ref, m_ref, l_ref, acc_ref, fq_ref):
    ki = pl.program_id(1)
    last = pl.num_programs(1) - 1
    t = q_ref.shape[0]
    tk = fp_ref.shape[1]
    heads = range(HEADS)
    hs = lambda hd: slice(hd * HEAD_DIM, (hd + 1) * HEAD_DIM)
    rows = lax.broadcasted_iota(jnp.int32, (t, t), 0)
    cols = lax.broadcasted_iota(jnp.int32, (t, t), 1)

    @pl.when(ki == 0)
    def _():
        m_ref[...] = jnp.full(m_ref.shape, NEG_INF, F32)
        l_ref[...] = jnp.zeros(l_ref.shape, F32)
        acc_ref[...] = jnp.zeros(acc_ref.shape, F32)
        f_new = (fl_ref[...] + fn_ref[...]) * LOG2E
        for hd in heads:
            fq_ref[hd] = _row_to_col(f_new[hd:hd + 1, :])

    def update(with_new):
        f_new = (fl_ref[...] + fn_ref[...]) * LOG2E
        f_past = fp_ref[...] * LOG2E
        s = []
        for hd in heads:
            q = q_ref[:, hs(hd)]
            k = ck_ref[pl.ds(hd, tk, stride=HEADS), :].astype(BF16)
            segs = [_dot_nt(q, k) - f_past[hd:hd + 1, :]]
            if with_new:
                sn = _dot_nt(q, kn_ref[:, hs(hd)]) - f_new[hd:hd + 1, :]
                segs.append(jnp.where(cols <= rows, sn, NEG_INF))
            s.append(segs)
        fq = [fq_ref[hd] for hd in heads]
        m_old = [m_ref[hd] for hd in heads]
        m_new = [jnp.maximum(m_old[hd], functools.reduce(
            jnp.maximum, [jnp.max(x, axis=1, keepdims=True) for x in s[hd]]) + fq[hd]) for hd in heads]
        p = [[jnp.exp2(x - (m_new[hd] - fq[hd])) for x in s[hd]] for hd in heads]
        alpha = [jnp.exp2(m_old[hd] - m_new[hd]) for hd in heads]
        l_new = [alpha[hd] * l_ref[hd] + sum(jnp.sum(x, axis=1, keepdims=True) for x in p[hd]) for hd in heads]
        for hd in heads:
            v = cv_ref[pl.ds(hd, tk, stride=HEADS), :].astype(BF16)
            pv = _dot(p[hd][0].astype(BF16), v)
            if with_new:
                pv = pv + _dot(p[hd][1].astype(BF16), vn_ref[:, hs(hd)])
            acc = alpha[hd] * acc_ref[hd] + pv
            if with_new:
                og_ref[:, hs(hd)] = (acc / l_new[hd] * sg_ref[:, hs(hd)]).astype(BF16)
            else:
                m_ref[hd] = m_new[hd]
                l_ref[hd] = l_new[hd]
                acc_ref[hd] = acc

    @pl.when(ki != last)
    def _():
        update(False)

    @pl.when(ki == last)
    def _():
        update(True)


def _sample_attn(q16, cache_k, cache_v, layer, k16, v16, sg, f_past, f_last, f_new, bsz, t):
    n, width = q16.shape
    p_len = f_past.shape[2]
    tk = SAMPLE_KV_BLOCK
    new = pl.BlockSpec((t, width), lambda b, k: (b, 0))
    past = pl.BlockSpec((None, None, tk * HEADS, HEAD_DIM), lambda b, k: (layer, b, k, 0))
    return pl.pallas_call(
        _sample_attn_kernel,
        grid=(bsz, p_len // tk),
        in_specs=[new, past, past, new, new, new,
                  pl.BlockSpec((None, HEADS, tk), lambda b, k: (b, 0, k)),
                  pl.BlockSpec((None, HEADS, 1), lambda b, k: (b, 0, 0)),
                  pl.BlockSpec((None, HEADS, t), lambda b, k: (b, 0, 0))],
        out_specs=new,
        out_shape=jax.ShapeDtypeStruct((n, width), BF16),
        scratch_shapes=[pltpu.VMEM((HEADS, t, 1), F32), pltpu.VMEM((HEADS, t, 1), F32),
                        pltpu.VMEM((HEADS, t, HEAD_DIM), F32), pltpu.VMEM((HEADS, t, 1), F32)],
        compiler_params=_params(2),
        name="sample_attn",
    )(q16, cache_k, cache_v, k16, v16, sg, f_past, f_last, f_new)


def _gmlp_kernel(x_ref, ap_ref, wp_ref, g_ref, win_ref, vg_ref, mix_ref, bias_ref, wout_ref, *rest, emit_v):
    if emit_v:
        o_ref, vn_ref, a_ref = rest
    else:
        o_ref, a_ref = rest
    tm = x_ref.shape[0]
    bw = wout_ref.shape[0]
    gd = bw // GROUPS
    x = x_ref[...] + _dot(ap_ref[...], wp_ref[...])
    h = _rmsnorm(x, g_ref[...]).astype(BF16)
    v = _dot(h, win_ref[:, bw:bw + gd])
    for gi in range(GROUPS):
        sl = slice(gi * gd, (gi + 1) * gd)
        u = _dot(h, win_ref[:, sl])
        gate = _dot(h, win_ref[:, 2 * bw + gi * gd:2 * bw + (gi + 1) * gd])
        vn = _rmsnorm(v, vg_ref[:, sl])
        if gi + 1 < GROUPS:
            v = _dot(h, win_ref[:, bw + (gi + 1) * gd:bw + (gi + 2) * gd])
        if emit_v:
            vn_ref[:, sl] = vn
        vn16 = vn.astype(BF16)
        mixed = jnp.concatenate(
            [_dot(mix_ref[gi], vn16[ci * MLP_CHUNK:(ci + 1) * MLP_CHUNK, :]) + bias_ref[:, gi:gi + 1]
             for ci in range(tm // MLP_CHUNK)], axis=0)
        a_ref[:, sl] = (u * mixed * jax.nn.silu(gate)).astype(BF16)
    o_ref[...] = x + _dot(a_ref[...], wout_ref[...])


def _gmlp_layer(x, a_prev, w_prev, g, win, vg, mix, bias, wout, layer, emit_v):
    n, d = x.shape
    bw = wout.shape[1]
    tm = GMLP_ROW_TILE
    row = lambda cols: pl.BlockSpec((tm, cols), lambda i: (i, 0))
    out_shape = [jax.ShapeDtypeStruct((n, d), F32)]
    out_specs = [row(d)]
    if emit_v:
        out_shape.append(jax.ShapeDtypeStruct((n, bw), F32))
        out_specs.append(row(bw))
    res = pl.pallas_call(
        functools.partial(_gmlp_kernel, emit_v=emit_v),
        grid=(n // tm,),
        in_specs=[row(d), row(a_prev.shape[1]), _layer_spec(w_prev.shape, layer),
                  _const_spec((1, d)), _layer_spec(win.shape, layer), _const_spec((1, bw)),
                  _const_spec(mix.shape), _const_spec(bias.shape), _layer_spec(wout.shape, layer)],
        out_specs=tuple(out_specs),
        out_shape=tuple(out_shape),
        scratch_shapes=[pltpu.VMEM((tm, bw), BF16)],
        compiler_params=_params(1),
        name="gmlp_layer",
    )(x, a_prev, w_prev, g.reshape(1, d), win, vg.reshape(1, bw), mix, bias, wout)
    return res if emit_v else res[0]


def _sgu_mask():
    c = jnp.arange(MLP_CHUNK) // STREAM_CHUNK
    return c[None, :] <= c[:, None]


def kernel(x_prompt, x_sample, cache_k, cache_v, cache_logf, norm_g, w_in_a, b_f, q_g, k_g, w_out_a,
           w_in_b, v_g, ws, bs, w_out_b):
    bsz, s_len, d = x_prompt.shape
    dbsz, t_new, _ = x_sample.shape
    p_len = cache_k.shape[2]
    width = HEADS * HEAD_DIM
    depth = norm_g.shape[0]
    assert depth % 2 == 0
    assert ROW_TILE == ATTN_BLOCK and HEADS % HEADS_PER_STEP == 0
    assert s_len % ATTN_BLOCK == 0 and s_len % CUMSUM_BLOCK == 0 and p_len % CUMSUM_BLOCK == 0
    assert p_len % SAMPLE_KV_BLOCK == 0 and MLP_CHUNK == 2 * t_new and GMLP_ROW_TILE % MLP_CHUNK == 0

    xp = x_prompt.reshape(bsz * s_len, d)
    xs = x_sample.reshape(dbsz * t_new, d)
    n_a = w_in_a.shape[0]
    ck = cache_k.reshape(n_a, dbsz, p_len * HEADS, HEAD_DIM)
    cv = cache_v.reshape(n_a, dbsz, p_len * HEADS, HEAD_DIM)
    kv_p, kv_s = None, None
    lp_l, ls_l, sgu_l = [], [], []
    w = w_in_a.astype(BF16)
    wf = jnp.pad(w_in_a[:, :, 4 * width:], ((0, 0), (0, 0), (0, HEAD_DIM - HEADS))).astype(BF16)
    wo = w_out_a.astype(BF16)
    win = w_in_b.astype(BF16)
    wout = w_out_b.astype(BF16)
    bw = wout.shape[1]
    for i in range(depth):
        j = i // 2
        if i % 2 == 0:

            q16, k_all, v_all, k16, v16, sg, logf = _fox_inproj(xp, norm_g[i], w, wf, b_f[j], q_g[j], k_g[j],
                                                                j, n_a, kv_p, v_transposed=True)
            kv_p = (k_all, v_all)
            nb = s_len // CUMSUM_BLOCK
            lt = jnp.transpose(logf.reshape(bsz, s_len, HEADS), (0, 2, 1))
            fcum = _cumsum_lanes(lt.reshape(bsz, HEADS, nb, CUMSUM_BLOCK), chained=True)
            fcum = fcum.reshape(bsz, HEADS, s_len // ATTN_BLOCK, 1, ATTN_BLOCK)
            og_p = _prompt_attn(q16, k16, v16, sg, fcum, bsz, s_len)
            lp_l.append(logf.reshape(bsz, s_len, HEADS))

            q16, k_all, v_all, k16, v16, sg, logf = _fox_inproj(xs, norm_g[i], w, wf, b_f[j], q_g[j], k_g[j],
                                                                j, n_a, kv_s, v_transposed=False)
            kv_s = (k_all, v_all)
            nbp = p_len // CUMSUM_BLOCK
            cl = jnp.transpose(cache_logf[j], (0, 2, 1)).reshape(dbsz, HEADS, nbp, CUMSUM_BLOCK)
            f_past = _cumsum_lanes(cl, chained=True).reshape(dbsz, HEADS, p_len)
            lt = jnp.transpose(logf.reshape(dbsz, t_new, HEADS), (0, 2, 1))
            f_new = _cumsum_lanes(lt.reshape(1, HEADS, dbsz, t_new), chained=False)
            f_new = f_new.reshape(dbsz, HEADS, t_new)
            og_s = _sample_attn(q16, ck, cv, j, k16, v16, sg, f_past, f_past[:, :, p_len - 1:], f_new, dbsz, t_new)
            ls_l.append(logf.reshape(dbsz, t_new, HEADS))
        else:
            mix_p = (ws[j] * _sgu_mask()[None]).astype(BF16)
            bias_p = jnp.transpose(bs[j])
            xp = _gmlp_layer(xp, og_p, wo, norm_g[i], win, v_g[j], mix_p, bias_p, wout, j, emit_v=False)
            a = (ws[j] * _sgu_mask()[None])[:, :t_new, :t_new]
            z = jnp.zeros_like(a)
            mix_s = jnp.concatenate([jnp.concatenate([a, z], axis=2),
                                     jnp.concatenate([z, a], axis=2)], axis=1).astype(BF16)
            bias_s = jnp.transpose(jnp.concatenate([bs[j][:, :t_new], bs[j][:, :t_new]], axis=1))
            xs, sv = _gmlp_layer(xs, og_s, wo, norm_g[i], win, v_g[j], mix_s, bias_s, wout, j, emit_v=True)
            sgu_l.append(sv.reshape(dbsz, t_new, bw))
    kv5 = lambda a, nb_, t_: a.reshape(n_a, nb_, t_, HEADS, HEAD_DIM)
    return (xp.reshape(bsz, s_len, d), xs.reshape(dbsz, t_new, d),
            kv5(kv_p[0], bsz, s_len), kv5(kv_p[1], bsz, s_len), jnp.stack(lp_l),
            kv5(kv_s[0], dbsz, t_new), kv5(kv_s[1], dbsz, t_new), jnp.stack(ls_l), jnp.stack(sgu_l))
```

```python
import functools

import jax
import jax.numpy as jnp
from jax import lax
from jax.experimental import pallas as pl
from jax.experimental.pallas import tpu as pltpu

F32 = jnp.float32
BF16 = jnp.bfloat16

EPS = 1e-6
NEG_INF = -1e30
LOG2E = 1.4426950408889634

HEADS = 8
HEAD_DIM = 128
Q_PRESCALE = HEAD_DIM ** -0.5 * LOG2E
GROUPS = 8
MLP_CHUNK = 128
STREAM_CHUNK = 64

V7X_LANES = 128
BF16_SUBLANE_ROWS = 16
V7X_VMEM_BYTES = 64 * 1024 * 1024
VMEM_LIMIT_BYTES = V7X_VMEM_BYTES - 8 * 1024 * 1024

ROW_TILE = 512
GMLP_ROW_TILE = 512
ATTN_BLOCK = 512
HEADS_PER_STEP = 4
SAMPLE_KV_BLOCK = 2048
CUMSUM_BLOCK = 256
CUMSUM_ROWS = 512


def _params(n_axes):
    return pltpu.CompilerParams(dimension_semantics=("arbitrary",) * n_axes,
                                vmem_limit_bytes=VMEM_LIMIT_BYTES)


def _const_spec(shape):
    nd = len(shape)
    return pl.BlockSpec(shape, lambda *_: (0,) * nd, pipeline_mode=pl.Buffered(1))


def _layer_spec(stacked_shape, layer):
    nd = len(stacked_shape) - 1
    return pl.BlockSpec((None,) + tuple(stacked_shape[1:]), lambda *_: (layer,) + (0,) * nd,
                        pipeline_mode=pl.Buffered(1))


def _rmsnorm(x, g):
    ms = jnp.mean(x * x, axis=-1, keepdims=True)
    return x * lax.rsqrt(ms + EPS) * g


def _dot(a, b):
    return jnp.dot(a, b, preferred_element_type=F32)


def _dot_nt(a, b):
    return lax.dot_general(a, b, (((1,), (1,)), ((), ())), preferred_element_type=F32)


def _fox_inproj_kernel(x_ref, g_ref, w_ref, wf_ref, bf_ref, qg_ref, kg_ref, *rest, layer, first, v_transposed):
    q16_ref, k32_ref, v32_ref, k16_ref, v16_ref, sg_ref, logf_ref = rest[-7:]
    if first:
        for slot in range(k32_ref.shape[0]):
            if slot != layer:
                k32_ref[slot] = jnp.zeros(k32_ref.shape[1:], F32)
                v32_ref[slot] = jnp.zeros(v32_ref.shape[1:], F32)
        k32_ref = k32_ref.at[layer]
        v32_ref = v32_ref.at[layer]
    tm = x_ref.shape[0]
    width = HEADS * HEAD_DIM
    h = _rmsnorm(x_ref[...], g_ref[...]).astype(BF16)

    zq = _dot(h, w_ref[:, 0:width])
    for hd in range(HEADS):
        sl = slice(hd * HEAD_DIM, (hd + 1) * HEAD_DIM)
        q16_ref[:, sl] = (_rmsnorm(zq[:, sl], qg_ref[...]) * Q_PRESCALE).astype(BF16)

    zk = _dot(h, w_ref[:, width:2 * width])
    for hd in range(HEADS):
        sl = slice(hd * HEAD_DIM, (hd + 1) * HEAD_DIM)
        kn = _rmsnorm(zk[:, sl], kg_ref[...])
        k32_ref[pl.ds(hd, tm, stride=HEADS), :] = kn
        k16_ref[:, sl] = kn.astype(BF16)

    zv = _dot(h, w_ref[:, 2 * width:3 * width])
    for hd in range(HEADS):
        sl = slice(hd * HEAD_DIM, (hd + 1) * HEAD_DIM)
        v32_ref[pl.ds(hd, tm, stride=HEADS), :] = zv[:, sl]
        if v_transposed:
            v16_ref[sl, :] = jnp.transpose(zv[:, sl]).astype(BF16)
    if not v_transposed:
        v16_ref[...] = zv.astype(BF16)

    zg = _dot(h, w_ref[:, 3 * width:4 * width])
    sg_ref[...] = jax.nn.silu(zg)

    zf = _dot(h, wf_ref[...])
    logf_ref[...] = jax.nn.log_sigmoid(zf[:, 0:HEADS] + bf_ref[...])


def _fox_inproj(x, g, w, wf, b_f, q_g, k_g, layer, n_layers, kv_prev, v_transposed):
    n, d = x.shape
    width = HEADS * HEAD_DIM
    tm = ROW_TILE
    first = kv_prev is None
    row = lambda cols: pl.BlockSpec((tm, cols), lambda i: (i, 0))
    if first:
        kv_spec = pl.BlockSpec((n_layers, tm * HEADS, HEAD_DIM), lambda i: (0, i, 0))
    else:
        kv_spec = pl.BlockSpec((None, tm * HEADS, HEAD_DIM), lambda i: (layer, i, 0))
    kv_shape = jax.ShapeDtypeStruct((n_layers, n * HEADS, HEAD_DIM), F32)
    if v_transposed:
        v16_shape = jax.ShapeDtypeStruct((n // tm, width, tm), BF16)
        v16_spec = pl.BlockSpec((None, width, tm), lambda i: (i, 0, 0))
    else:
        v16_shape = jax.ShapeDtypeStruct((n, width), BF16)
        v16_spec = row(width)
    out_shape = (
        jax.ShapeDtypeStruct((n, width), BF16),
        kv_shape,
        kv_shape,
        jax.ShapeDtypeStruct((n, width), BF16),
        v16_shape,
        jax.ShapeDtypeStruct((n, width), F32),
        jax.ShapeDtypeStruct((n, HEADS), F32),
    )
    in_specs = [row(d), _const_spec((1, d)), _layer_spec(w.shape, layer), _layer_spec(wf.shape, layer),
                _const_spec((1, HEADS)), _const_spec((1, HEAD_DIM)), _const_spec((1, HEAD_DIM))]
    args = [x, g.reshape(1, d), w, wf, b_f.reshape(1, HEADS), q_g.reshape(1, HEAD_DIM), k_g.reshape(1, HEAD_DIM)]
    aliases = {}
    if not first:
        in_specs += [pl.BlockSpec(memory_space=pl.ANY)] * 2
        aliases = {len(args): 1, len(args) + 1: 2}
        args += list(kv_prev)
    return pl.pallas_call(
        functools.partial(_fox_inproj_kernel, layer=layer, first=first, v_transposed=v_transposed),
        grid=(n // tm,),
        in_specs=in_specs,
        out_specs=(row(width), kv_spec, kv_spec, row(width), v16_spec, row(width), row(HEADS)),
        out_shape=out_shape,
        input_output_aliases=aliases,
        compiler_params=_params(1),
        name="fox_inproj",
    )(*args)


def _split3(x):
    hi = x.astype(BF16)
    r1 = x - hi.astype(F32)
    mid = r1.astype(BF16)
    lo = (r1 - mid.astype(F32)).astype(BF16)
    return hi, mid, lo


def _cumsum_kernel(x_ref, o_ref, *, chain_shift):
    rows, lb = x_ref.shape
    r = lax.broadcasted_iota(jnp.int32, (lb, lb), 0)
    c = lax.broadcasted_iota(jnp.int32, (lb, lb), 1)
    upper = (r <= c).astype(BF16)
    hi, mid, lo = _split3(x_ref[...])
    cs = _dot(hi, upper) + _dot(mid, upper) + _dot(lo, upper)
    if chain_shift:
        rb = lax.broadcasted_iota(jnp.int32, (rows, rows), 0)
        cb = lax.broadcasted_iota(jnp.int32, (rows, rows), 1)
        tot = jnp.broadcast_to(cs[:, lb - 1:lb], (rows, rows))
        tot_row = jnp.sum(jnp.where(rb == cb, tot, 0.0), axis=0, keepdims=True)
        earlier = (cb < rb) & (lax.shift_right_logical(cb, chain_shift) == lax.shift_right_logical(rb, chain_shift))
        offs = jnp.sum(jnp.where(earlier, jnp.broadcast_to(tot_row, (rows, rows)), 0.0),
                       axis=1, keepdims=True)
        cs = cs + offs
    o_ref[...] = cs


def _cumsum_lanes(x, chained):
    bsz, nh, nb, lb = x.shape
    chain_shift = (nb.bit_length() - 1) if chained else 0
    assert nb == 1 << (nb.bit_length() - 1)
    total = bsz * nh * nb
    rows = min(total, CUMSUM_ROWS)
    assert total % rows == 0 and rows % nb == 0
    spec = pl.BlockSpec((None, rows, lb), lambda b: (b, 0, 0))
    out = pl.pallas_call(
        functools.partial(_cumsum_kernel, chain_shift=chain_shift),
        grid=(total // rows,),
        in_specs=[spec],
        out_specs=spec,
        out_shape=jax.ShapeDtypeStruct((total // rows, rows, lb), F32),
        compiler_params=_params(1),
        name="logf_cumsum",
    )(x.reshape(total // rows, rows, lb))
    return out.reshape(x.shape)


def _row_to_col(row):
    n = row.shape[1]
    c = min(n, V7X_LANES)
    eye = lax.broadcasted_iota(jnp.int32, (c, c), 0) == lax.broadcasted_iota(jnp.int32, (c, c), 1)
    pieces = [jnp.sum(jnp.where(eye, jnp.broadcast_to(row[:, i:i + c], (c, c)), 0.0), axis=1, keepdims=True)
              for i in range(0, n, c)]
    return pieces[0] if len(pieces) == 1 else jnp.concatenate(pieces, axis=0)


def _prompt_attn_kernel(q_ref, k_ref, vt_ref, sg_ref, f_ref, og_ref, m_s, l_s, acc_s, kx_s):
    t = ATTN_BLOCK
    hp, nblk = f_ref.shape[0], f_ref.shape[1]
    heads = range(hp)
    hs = lambda h: slice(h * HEAD_DIM, (h + 1) * HEAD_DIM)
    key_i = lax.broadcasted_iota(jnp.int32, (t, t), 0)
    qry_i = lax.broadcasted_iota(jnp.int32, (t, t), 1)
    causal = key_i <= qry_i
    ones_rows = jnp.ones((BF16_SUBLANE_ROWS, t), BF16)
    sub_i = lax.broadcasted_iota(jnp.int32, (8, t), 0)
    lane_i = lax.broadcasted_iota(jnp.int32, (t, HEAD_DIM), 1)
    q_extra = jnp.where(lane_i < 3, 1.0, 0.0).astype(BF16)
    for h in heads:
        for blk in range(nblk):
            hi, mid, lo = _split3(f_ref[h, blk] * -LOG2E)
            rows3 = jnp.where(sub_i == 0, hi.astype(F32),
                              jnp.where(sub_i == 1, mid.astype(F32), jnp.where(sub_i == 2, lo.astype(F32), 0.0)))
            padded = jnp.concatenate([rows3, jnp.zeros((HEAD_DIM - 8, t), F32)], axis=0)
            kx_s[h, blk] = jnp.transpose(padded).astype(BF16)

    def step(q, fq, k_of, vt_of, kx_of, masked):
        ut = [_dot_nt(jnp.concatenate([k_of(h), kx_of(h)], axis=1), jnp.concatenate([q[h], q_extra], axis=1))
              for h in heads]
        if masked:
            ut = [jnp.where(causal, x, NEG_INF) for x in ut]
        m_old = [m_s[h] for h in heads]
        m_new = [jnp.maximum(m_old[h], jnp.max(ut[h], axis=0, keepdims=True) + fq[h]) for h in heads]
        pt = [jnp.exp2(ut[h] - (m_new[h] - fq[h])) for h in heads]
        alpha = [jnp.exp2(m_old[h] - m_new[h]) for h in heads]
        for h in heads:
            pv = _dot(jnp.concatenate([vt_of(h), ones_rows], axis=0), pt[h].astype(BF16))
            m_s[h] = m_new[h]
            l_s[h] = alpha[h] * l_s[h] + pv[HEAD_DIM:HEAD_DIM + 1, :]
            acc_s[h] = alpha[h] * acc_s[h] + pv[:HEAD_DIM, :]

    for qi in range(nblk):
        qs = slice(qi * t, (qi + 1) * t)
        q = [q_ref[qs, hs(h)] for h in heads]
        fq = [f_ref[h, qi] * LOG2E for h in heads]
        m_s[...] = jnp.full(m_s.shape, NEG_INF, F32)
        l_s[...] = jnp.zeros(l_s.shape, F32)
        acc_s[...] = jnp.zeros(acc_s.shape, F32)

        def body(ki, carry, q=q, fq=fq):
            start = pl.multiple_of(ki * t, t)
            step(q, fq, lambda h: k_ref[pl.ds(start, t), hs(h)], lambda h: vt_ref[ki, hs(h), :],
                 lambda h: kx_s[h, ki], False)
            return carry

        lax.fori_loop(0, qi, body, 0)
        step(q, fq, lambda h: k_ref[qs, hs(h)], lambda h: vt_ref[qi, hs(h), :], lambda h: kx_s[h, qi], True)
        for h in heads:
            o = jnp.transpose(acc_s[h] / l_s[h])
            og_ref[qs, hs(h)] = (o * sg_ref[qs, hs(h)]).astype(BF16)


def _prompt_attn(q16, k16, vt16, sg, fcum, bsz, s_len):
    t = ATTN_BLOCK
    hp = HEADS_PER_STEP
    nblk = s_len // t
    n, width = q16.shape
    blk = pl.BlockSpec((s_len, hp * HEAD_DIM), lambda b, g: (b, g))
    return pl.pallas_call(
        _prompt_attn_kernel,
        grid=(bsz, HEADS // hp),
        in_specs=[blk, blk,
                  pl.BlockSpec((nblk, hp * HEAD_DIM, t), lambda b, g: (b, g, 0)),
                  blk,
                  pl.BlockSpec((None, hp, nblk, 1, t), lambda b, g: (b, g, 0, 0, 0))],
        out_specs=blk,
        out_shape=jax.ShapeDtypeStruct((n, width), BF16),
        scratch_shapes=[pltpu.VMEM((hp, 1, t), F32), pltpu.VMEM((hp, 1, t), F32),
                        pltpu.VMEM((hp, HEAD_DIM, t), F32), pltpu.VMEM((hp, nblk, t, HEAD_DIM), BF16)],
        compiler_params=_params(2),
        name="prompt_attn",
    )(q16, k16, vt16, sg, fcum)


def _sample_attn_kernel(q_ref, ck_ref, cv_ref, kn_ref, vn_ref, sg_ref, fp_ref, fl_ref, fn_ref,
                        og_ref, m_ref, l_ref, acc_ref, fq_ref):
    ki = pl.program_id(1)
    last = pl.num_programs(1) - 1
    t = q_ref.shape[0]
    tk = fp_ref.shape[1]
    heads = range(HEADS)
    hs = lambda hd: slice(hd * HEAD_DIM, (hd + 1) * HEAD_DIM)
    rows = lax.broadcasted_iota(jnp.int32, (t, t), 0)
    cols = lax.broadcasted_iota(jnp.int32, (t, t), 1)

    @pl.when(ki == 0)
    def _():
        m_ref[...] = jnp.full(m_ref.shape, NEG_INF, F32)
        l_ref[...] = jnp.zeros(l_ref.shape, F32)
        acc_ref[...] = jnp.zeros(acc_ref.shape, F32)
        f_new = (fl_ref[...] + fn_ref[...]) * LOG2E
        for hd in heads:
            fq_ref[hd] = _row_to_col(f_new[hd:hd + 1, :])

    def update(with_new):
        f_new = (fl_ref[...] + fn_ref[...]) * LOG2E
        f_past = fp_ref[...] * LOG2E
        s = []
        for hd in heads:
            q = q_ref[:, hs(hd)]
            k = ck_ref[pl.ds(hd, tk, stride=HEADS), :].astype(BF16)
            segs = [_dot_nt(q, k) - f_past[hd:hd + 1, :]]
            if with_new:
                sn = _dot_nt(q, kn_ref[:, hs(hd)]) - f_new[hd:hd + 1, :]
                segs.append(jnp.where(cols <= rows, sn, NEG_INF))
            s.append(segs)
        fq = [fq_ref[hd] for hd in heads]
        m_old = [m_ref[hd] for hd in heads]
        m_new = [jnp.maximum(m_old[hd], functools.reduce(
            jnp.maximum, [jnp.max(x, axis=1, keepdims=True) for x in s[hd]]) + fq[hd]) for hd in heads]
        p = [[jnp.exp2(x - (m_new[hd] - fq[hd])) for x in s[hd]] for hd in heads]
        alpha = [jnp.exp2(m_old[hd] - m_new[hd]) for hd in heads]
        l_new = [alpha[hd] * l_ref[hd] + sum(jnp.sum(x, axis=1, keepdims=True) for x in p[hd]) for hd in heads]
        for hd in heads:
            v = cv_ref[pl.ds(hd, tk, stride=HEADS), :].astype(BF16)
            pv = _dot(p[hd][0].astype(BF16), v)
            if with_new:
                pv = pv + _dot(p[hd][1].astype(BF16), vn_ref[:, hs(hd)])
            acc = alpha[hd] * acc_ref[hd] + pv
            if with_new:
                og_ref[:, hs(hd)] = (acc / l_new[hd] * sg_ref[:, hs(hd)]).astype(BF16)
            else:
                m_ref[hd] = m_new[hd]
                l_ref[hd] = l_new[hd]
                acc_ref[hd] = acc

    @pl.when(ki != last)
    def _():
        update(False)

    @pl.when(ki == last)
    def _():
        update(True)


def _sample_attn(q16, cache_k, cache_v, layer, k16, v16, sg, f_past, f_last, f_new, bsz, t):
    n, width = q16.shape
    p_len = f_past.shape[2]
    tk = SAMPLE_KV_BLOCK
    new = pl.BlockSpec((t, width), lambda b, k: (b, 0))
    past = pl.BlockSpec((None, None, tk * HEADS, HEAD_DIM), lambda b, k: (layer, b, k, 0))
    return pl.pallas_call(
        _sample_attn_kernel,
        grid=(bsz, p_len // tk),
        in_specs=[new, past, past, new, new, new,
                  pl.BlockSpec((None, HEADS, tk), lambda b, k: (b, 0, k)),
                  pl.BlockSpec((None, HEADS, 1), lambda b, k: (b, 0, 0)),
                  pl.BlockSpec((None, HEADS, t), lambda b, k: (b, 0, 0))],
        out_specs=new,
        out_shape=jax.ShapeDtypeStruct((n, width), BF16),
        scratch_shapes=[pltpu.VMEM((HEADS, t, 1), F32), pltpu.VMEM((HEADS, t, 1), F32),
                        pltpu.VMEM((HEADS, t, HEAD_DIM), F32), pltpu.VMEM((HEADS, t, 1), F32)],
        compiler_params=_params(2),
        name="sample_attn",
    )(q16, cache_k, cache_v, k16, v16, sg, f_past, f_last, f_new)


def _gmlp_kernel(x_ref, ap_ref, wp_ref, g_ref, win_ref, vg_ref, mix_ref, bias_ref, wout_ref, *rest, emit_v):
    if emit_v:
        o_ref, vn_ref, a_ref = rest
    else:
        o_ref, a_ref = rest
    tm = x_ref.shape[0]
    bw = wout_ref.shape[0]
    gd = bw // GROUPS
    x = x_ref[...] + _dot(ap_ref[...], wp_ref[...])
    h = _rmsnorm(x, g_ref[...]).astype(BF16)
    v = _dot(h, win_ref[:, bw:bw + gd])
    for gi in range(GROUPS):
        sl = slice(gi * gd, (gi + 1) * gd)
        u = _dot(h, win_ref[:, sl])
        gate = _dot(h, win_ref[:, 2 * bw + gi * gd:2 * bw + (gi + 1) * gd])
        vn = _rmsnorm(v, vg_ref[:, sl])
        if gi + 1 < GROUPS:
            v = _dot(h, win_ref[:, bw + (gi + 1) * gd:bw + (gi + 2) * gd])
        if emit_v:
            vn_ref[:, sl] = vn
        vn16 = vn.astype(BF16)
        mixed = jnp.concatenate(
            [_dot(mix_ref[gi], vn16[ci * MLP_CHUNK:(ci + 1) * MLP_CHUNK, :]) + bias_ref[:, gi:gi + 1]
             for ci in range(tm // MLP_CHUNK)], axis=0)
        a_ref[:, sl] = (u * mixed * jax.nn.silu(gate)).astype(BF16)
    o_ref[...] = x + _dot(a_ref[...], wout_ref[...])


def _gmlp_layer(x, a_prev, w_prev, g, win, vg, mix, bias, wout, layer, emit_v):
    n, d = x.shape
    bw = wout.shape[1]
    tm = GMLP_ROW_TILE
    row = lambda cols: pl.BlockSpec((tm, cols), lambda i: (i, 0))
    out_shape = [jax.ShapeDtypeStruct((n, d), F32)]
    out_specs = [row(d)]
    if emit_v:
        out_shape.append(jax.ShapeDtypeStruct((n, bw), F32))
        out_specs.append(row(bw))
    res = pl.pallas_call(
        functools.partial(_gmlp_kernel, emit_v=emit_v),
        grid=(n // tm,),
        in_specs=[row(d), row(a_prev.shape[1]), _layer_spec(w_prev.shape, layer),
                  _const_spec((1, d)), _layer_spec(win.shape, layer), _const_spec((1, bw)),
                  _const_spec(mix.shape), _const_spec(bias.shape), _layer_spec(wout.shape, layer)],
        out_specs=tuple(out_specs),
        out_shape=tuple(out_shape),
        scratch_shapes=[pltpu.VMEM((tm, bw), BF16)],
        compiler_params=_params(1),
        name="gmlp_layer",
    )(x, a_prev, w_prev, g.reshape(1, d), win, vg.reshape(1, bw), mix, bias, wout)
    return res if emit_v else res[0]


def _sgu_mask():
    c = jnp.arange(MLP_CHUNK) // STREAM_CHUNK
    return c[None, :] <= c[:, None]


def kernel(x_prompt, x_sample, cache_k, cache_v, cache_logf, norm_g, w_in_a, b_f, q_g, k_g, w_out_a,
           w_in_b, v_g, ws, bs, w_out_b):
    bsz, s_len, d = x_prompt.shape
    dbsz, t_new, _ = x_sample.shape
    p_len = cache_k.shape[2]
    width = HEADS * HEAD_DIM
    depth = norm_g.shape[0]
    assert depth % 2 == 0
    assert ROW_TILE == ATTN_BLOCK and HEADS % HEADS_PER_STEP == 0
    assert s_len % ATTN_BLOCK == 0 and s_len % CUMSUM_BLOCK == 0 and p_len % CUMSUM_BLOCK == 0
    assert p_len % SAMPLE_KV_BLOCK == 0 and MLP_CHUNK == 2 * t_new and GMLP_ROW_TILE % MLP_CHUNK == 0

    xp = x_prompt.reshape(bsz * s_len, d)
    xs = x_sample.reshape(dbsz * t_new, d)
    n_a = w_in_a.shape[0]
    ck = cache_k.reshape(n_a, dbsz, p_len * HEADS, HEAD_DIM)
    cv = cache_v.reshape(n_a, dbsz, p_len * HEADS, HEAD_DIM)
    kv_p, kv_s = None, None
    lp_l, ls_l, sgu_l = [], [], []
    w = w_in_a.astype(BF16)
    wf = jnp.pad(w_in_a[:, :, 4 * width:], ((0, 0), (0, 0), (0, HEAD_DIM - HEADS))).astype(BF16)
    wo = w_out_a.astype(BF16)
    win = w_in_b.astype(BF16)
    wout = w_out_b.astype(BF16)
    bw = wout.shape[1]
    for i in range(depth):
        j = i // 2
        if i % 2 == 0:

            q16, k_all, v_all, k16, v16, sg, logf = _fox_inproj(xp, norm_g[i], w, wf, b_f[j], q_g[j], k_g[j],
                                                                j, n_a, kv_p, v_transposed=True)
            kv_p = (k_all, v_all)
            nb = s_len // CUMSUM_BLOCK
            lt = jnp.transpose(logf.reshape(bsz, s_len, HEADS), (0, 2, 1))
            fcum = _cumsum_lanes(lt.reshape(bsz, HEADS, nb, CUMSUM_BLOCK), chained=True)
            fcum = fcum.reshape(bsz, HEADS, s_len // ATTN_BLOCK, 1, ATTN_BLOCK)
            og_p = _prompt_attn(q16, k16, v16, sg, fcum, bsz, s_len)
            lp_l.append(logf.reshape(bsz, s_len, HEADS))

            q16, k_all, v_all, k16, v16, sg, logf = _fox_inproj(xs, norm_g[i], w, wf, b_f[j], q_g[j], k_g[j],
                                                                j, n_a, kv_s, v_transposed=False)
            kv_s = (k_all, v_all)
            nbp = p_len // CUMSUM_BLOCK
            cl = jnp.transpose(cache_logf[j], (0, 2, 1)).reshape(dbsz, HEADS, nbp, CUMSUM_BLOCK)
            f_past = _cumsum_lanes(cl, chained=True).reshape(dbsz, HEADS, p_len)
            lt = jnp.transpose(logf.reshape(dbsz, t_new, HEADS), (0, 2, 1))
            f_new = _cumsum_lanes(lt.reshape(1, HEADS, dbsz, t_new), chained=False)
            f_new = f_new.reshape(dbsz, HEADS, t_new)
            og_s = _sample_attn(q16, ck, cv, j, k16, v16, sg, f_past, f_past[:, :, p_len - 1:], f_new, dbsz, t_new)
            ls_l.append(logf.reshape(dbsz, t_new, HEADS))
        else:
            mix_p = (ws[j] * _sgu_mask()[None]).astype(BF16)
            bias_p = jnp.transpose(bs[j])
            xp = _gmlp_layer(xp, og_p, wo, norm_g[i], win, v_g[j], mix_p, bias_p, wout, j, emit_v=False)
            a = (ws[j] * _sgu_mask()[None])[:, :t_new, :t_new]
            z = jnp.zeros_like(a)
            mix_s = jnp.concatenate([jnp.concatenate([a, z], axis=2),
                                     jnp.concatenate([z, a], axis=2)], axis=1).astype(BF16)
            bias_s = jnp.transpose(jnp.concatenate([bs[j][:, :t_new], bs[j][:, :t_new]], axis=1))
            xs, sv = _gmlp_layer(xs, og_s, wo, norm_g[i], win, v_g[j], mix_s, bias_s, wout, j, emit_v=True)
            sgu_l.append(sv.reshape(dbsz, t_new, bw))
    kv5 = lambda a, nb_, t_: a.reshape(n_a, nb_, t_, HEADS, HEAD_DIM)
    return (xp.reshape(bsz, s_len, d), xs.reshape(dbsz, t_new, d),
            kv5(kv_p[0], bsz, s_len), kv5(kv_p[1], bsz, s_len), jnp.stack(lp_l),
            kv5(kv_s[0], dbsz, t_new), kv5(kv_s[1], dbsz, t_new), jnp.stack(ls_l), jnp.stack(sgu_l))
```

```python
import functools

import jax
import jax.numpy as jnp
from jax import lax
from jax.experimental import pallas as pl
from jax.experimental.pallas import tpu as pltpu

F32 = jnp.float32
BF16 = jnp.bfloat16

EPS = 1e-6
NEG_INF = -1e30
LOG2E = 1.4426950408889634

HEADS = 8
HEAD_DIM = 128
Q_PRESCALE = HEAD_DIM ** -0.5 * LOG2E
GROUPS = 8
MLP_CHUNK = 128
STREAM_CHUNK = 64

V7X_LANES = 128
V7X_SUBLANES = 8
BF16_SUBLANE_ROWS = 16
SPLIT_PARTS = 3
V7X_VMEM_BYTES = 64 * 1024 * 1024
VMEM_LIMIT_BYTES = V7X_VMEM_BYTES - 8 * 1024 * 1024

ROW_TILE = 512
GMLP_ROW_TILE = 512
ATTN_BLOCK = 512
HEADS_PER_STEP = 4
SAMPLE_KV_BLOCK = 2048
CUMSUM_BLOCK = 256
CUMSUM_ROWS = 512


def _params(n_axes):
    return pltpu.CompilerParams(dimension_semantics=("arbitrary",) * n_axes,
                                vmem_limit_bytes=VMEM_LIMIT_BYTES)


def _const_spec(shape):
    nd = len(shape)
    return pl.BlockSpec(shape, lambda *_: (0,) * nd, pipeline_mode=pl.Buffered(1))


def _layer_spec(stacked_shape, layer):
    nd = len(stacked_shape) - 1
    return pl.BlockSpec((None,) + tuple(stacked_shape[1:]), lambda *_: (layer,) + (0,) * nd,
                        pipeline_mode=pl.Buffered(1))


def _rmsnorm(x, g):
    ms = jnp.mean(x * x, axis=-1, keepdims=True)
    return x * lax.rsqrt(ms + EPS) * g


def _dot(a, b):
    return jnp.dot(a, b, preferred_element_type=F32)


def _dot_nt(a, b):
    return lax.dot_general(a, b, (((1,), (1,)), ((), ())), preferred_element_type=F32)


def _fox_inproj_kernel(x_ref, g_ref, w_ref, wf_ref, bf_ref, qg_ref, kg_ref, *rest, layer, first, v_transposed):
    q16_ref, k32_ref, v32_ref, k16_ref, v16_ref, sg_ref, logf_ref = rest[-7:]
    if first:
        for slot in range(k32_ref.shape[0]):
            if slot != layer:
                k32_ref[slot] = jnp.zeros(k32_ref.shape[1:], F32)
                v32_ref[slot] = jnp.zeros(v32_ref.shape[1:], F32)
        k32_ref = k32_ref.at[layer]
        v32_ref = v32_ref.at[layer]
    tm = x_ref.shape[0]
    width = HEADS * HEAD_DIM
    half = tm // 2
    for r0 in (0, half):
        rs = slice(r0, r0 + half)
        h = _rmsnorm(x_ref[rs, :], g_ref[...]).astype(BF16)

        zq = _dot(h, w_ref[:, 0:width])
        for hd in range(HEADS):
            sl = slice(hd * HEAD_DIM, (hd + 1) * HEAD_DIM)
            q16_ref[rs, sl] = (_rmsnorm(zq[:, sl], qg_ref[...]) * Q_PRESCALE).astype(BF16)

        zk = _dot(h, w_ref[:, width:2 * width])
        for hd in range(HEADS):
            sl = slice(hd * HEAD_DIM, (hd + 1) * HEAD_DIM)
            kn = _rmsnorm(zk[:, sl], kg_ref[...])
            k32_ref[pl.ds(r0 * HEADS + hd, half, stride=HEADS), :] = kn
            k16_ref[rs, sl] = kn.astype(BF16)

        zv = _dot(h, w_ref[:, 2 * width:3 * width])
        for hd in range(HEADS):
            sl = slice(hd * HEAD_DIM, (hd + 1) * HEAD_DIM)
            v32_ref[pl.ds(r0 * HEADS + hd, half, stride=HEADS), :] = zv[:, sl]
            if v_transposed:
                v16_ref[sl, rs] = jnp.transpose(zv[:, sl]).astype(BF16)
        if not v_transposed:
            v16_ref[rs, :] = zv.astype(BF16)

        zg = _dot(h, w_ref[:, 3 * width:4 * width])
        sg_ref[rs, :] = jax.nn.silu(zg)

        zf = _dot(h, wf_ref[...])
        logf_ref[rs, :] = jax.nn.log_sigmoid(zf[:, 0:HEADS] + bf_ref[...])


def _fox_inproj(x, g, w, wf, b_f, q_g, k_g, layer, n_layers, kv_prev, v_transposed):
    n, d = x.shape
    width = HEADS * HEAD_DIM
    tm = ROW_TILE
    first = kv_prev is None
    row = lambda cols: pl.BlockSpec((tm, cols), lambda i: (i, 0))
    if first:
        kv_spec = pl.BlockSpec((n_layers, tm * HEADS, HEAD_DIM), lambda i: (0, i, 0))
    else:
        kv_spec = pl.BlockSpec((None, tm * HEADS, HEAD_DIM), lambda i: (layer, i, 0))
    kv_shape = jax.ShapeDtypeStruct((n_layers, n * HEADS, HEAD_DIM), F32)
    if v_transposed:
        v16_shape = jax.ShapeDtypeStruct((n // tm, width, tm), BF16)
        v16_spec = pl.BlockSpec((None, width, tm), lambda i: (i, 0, 0))
    else:
        v16_shape = jax.ShapeDtypeStruct((n, width), BF16)
        v16_spec = row(width)
    out_shape = (
        jax.ShapeDtypeStruct((n, width), BF16),
        kv_shape,
        kv_shape,
        jax.ShapeDtypeStruct((n, width), BF16),
        v16_shape,
        jax.ShapeDtypeStruct((n, width), F32),
        jax.ShapeDtypeStruct((n, HEADS), F32),
    )
    in_specs = [row(d), _const_spec((1, d)), _layer_spec(w.shape, layer), _layer_spec(wf.shape, layer),
                _const_spec((1, HEADS)), _const_spec((1, HEAD_DIM)), _const_spec((1, HEAD_DIM))]
    args = [x, g.reshape(1, d), w, wf, b_f.reshape(1, HEADS), q_g.reshape(1, HEAD_DIM), k_g.reshape(1, HEAD_DIM)]
    aliases = {}
    if not first:
        in_specs += [pl.BlockSpec(memory_space=pl.ANY)] * 2
        aliases = {len(args): 1, len(args) + 1: 2}
        args += list(kv_prev)
    return pl.pallas_call(
        functools.partial(_fox_inproj_kernel, layer=layer, first=first, v_transposed=v_transposed),
        grid=(n // tm,),
        in_specs=in_specs,
        out_specs=(row(width), kv_spec, kv_spec, row(width), v16_spec, row(width), row(HEADS)),
        out_shape=out_shape,
        input_output_aliases=aliases,
        compiler_params=_params(1),
        name="fox_inproj",
    )(*args)


def _split3(x):
    hi = x.astype(BF16)
    r1 = x - hi.astype(F32)
    mid = r1.astype(BF16)
    lo = (r1 - mid.astype(F32)).astype(BF16)
    return hi, mid, lo


def _cumsum_kernel(x_ref, o_ref, *, chain_shift):
    rows, lb = x_ref.shape
    r = lax.broadcasted_iota(jnp.int32, (lb, lb), 0)
    c = lax.broadcasted_iota(jnp.int32, (lb, lb), 1)
    upper = (r <= c).astype(BF16)
    hi, mid, lo = _split3(x_ref[...])
    cs = _dot(hi, upper) + _dot(mid, upper) + _dot(lo, upper)
    if chain_shift:
        rb = lax.broadcasted_iota(jnp.int32, (rows, rows), 0)
        cb = lax.broadcasted_iota(jnp.int32, (rows, rows), 1)
        tot = jnp.broadcast_to(cs[:, lb - 1:lb], (rows, rows))
        tot_row = jnp.sum(jnp.where(rb == cb, tot, 0.0), axis=0, keepdims=True)
        earlier = (cb < rb) & (lax.shift_right_logical(cb, chain_shift) == lax.shift_right_logical(rb, chain_shift))
        offs = jnp.sum(jnp.where(earlier, jnp.broadcast_to(tot_row, (rows, rows)), 0.0),
                       axis=1, keepdims=True)
        cs = cs + offs
    o_ref[...] = cs


def _cumsum_lanes(x, chained):
    bsz, nh, nb, lb = x.shape
    chain_shift = (nb.bit_length() - 1) if chained else 0
    assert nb == 1 << (nb.bit_length() - 1)
    total = bsz * nh * nb
    rows = min(total, CUMSUM_ROWS)
    assert total % rows == 0 and rows % nb == 0
    spec = pl.BlockSpec((None, rows, lb), lambda b: (b, 0, 0))
    out = pl.pallas_call(
        functools.partial(_cumsum_kernel, chain_shift=chain_shift),
        grid=(total // rows,),
        in_specs=[spec],
        out_specs=spec,
        out_shape=jax.ShapeDtypeStruct((total // rows, rows, lb), F32),
        compiler_params=_params(1),
        name="logf_cumsum",
    )(x.reshape(total // rows, rows, lb))
    return out.reshape(x.shape)


def _row_to_col(row):
    n = row.shape[1]
    c = min(n, V7X_LANES)
    eye = lax.broadcasted_iota(jnp.int32, (c, c), 0) == lax.broadcasted_iota(jnp.int32, (c, c), 1)
    pieces = [jnp.sum(jnp.where(eye, jnp.broadcast_to(row[:, i:i + c], (c, c)), 0.0), axis=1, keepdims=True)
              for i in range(0, n, c)]
    return pieces[0] if len(pieces) == 1 else jnp.concatenate(pieces, axis=0)


def _prompt_attn_kernel(q_ref, k_ref, vt_ref, sg_ref, f_ref, og_ref, m_s, l_s, acc_s, kx_s):
    t = ATTN_BLOCK
    hp, nblk = f_ref.shape[0], f_ref.shape[1]
    heads = range(hp)
    hs = lambda h: slice(h * HEAD_DIM, (h + 1) * HEAD_DIM)
    key_i = lax.broadcasted_iota(jnp.int32, (t, t), 0)
    qry_i = lax.broadcasted_iota(jnp.int32, (t, t), 1)
    causal = key_i <= qry_i
    ones_rows = jnp.ones((BF16_SUBLANE_ROWS, t), BF16)
    sub_i = lax.broadcasted_iota(jnp.int32, (V7X_SUBLANES, t), 0)
    lane_i = lax.broadcasted_iota(jnp.int32, (t, HEAD_DIM), 1)
    q_extra = jnp.where(lane_i < SPLIT_PARTS, 1.0, 0.0).astype(BF16)
    for h in heads:
        for blk in range(nblk):
            hi, mid, lo = _split3(f_ref[h, blk] * -LOG2E)
            rows3 = jnp.where(sub_i == 0, hi.astype(F32),
                              jnp.where(sub_i == 1, mid.astype(F32), jnp.where(sub_i == 2, lo.astype(F32), 0.0)))
            padded = jnp.concatenate([rows3, jnp.zeros((HEAD_DIM - V7X_SUBLANES, t), F32)], axis=0)
            kx_s[h, blk] = jnp.transpose(padded).astype(BF16)

    def diag_step(q, fq, qi):
        h2 = t // 2
        lo, mid, hi = qi * t, qi * t + h2, (qi + 1) * t
        join = lambda a, b: jnp.concatenate([a[:, :h2], a[:, h2:] + b], axis=1)
        u1, u2 = [], []
        for h in heads:
            kx = kx_s[h, qi]
            qa = jnp.concatenate([q[h], q_extra], axis=1)
            x1 = _dot_nt(jnp.concatenate([k_ref[lo:mid, hs(h)], kx[:h2]], axis=1), qa)
            x2 = _dot_nt(jnp.concatenate([k_ref[mid:hi, hs(h)], kx[h2:]], axis=1), qa[h2:])
            u1.append(jnp.where(causal[:h2, :], x1, NEG_INF))
            u2.append(jnp.where(causal[:h2, :h2], x2, NEG_INF))
        m_old = [m_s[h] for h in heads]
        top = [jnp.max(u1[h], axis=0, keepdims=True) for h in heads]
        top = [jnp.concatenate([top[h][:, :h2], jnp.maximum(top[h][:, h2:], jnp.max(u2[h], axis=0, keepdims=True))],
                               axis=1) for h in heads]
        m_new = [jnp.maximum(m_old[h], top[h] + fq[h]) for h in heads]
        off = [m_new[h] - fq[h] for h in heads]
        p1 = [jnp.exp2(u1[h] - off[h]) for h in heads]
        p2 = [jnp.exp2(u2[h] - off[h][:, h2:]) for h in heads]
        alpha = [jnp.exp2(m_old[h] - m_new[h]) for h in heads]
        for h in heads:
            vt = vt_ref[qi, hs(h), :]
            pv1 = _dot(jnp.concatenate([vt[:, :h2], ones_rows[:, :h2]], axis=0), p1[h].astype(BF16))
            pv2 = _dot(jnp.concatenate([vt[:, h2:], ones_rows[:, :h2]], axis=0), p2[h].astype(BF16))
            pv = join(pv1, pv2)
            m_s[h] = m_new[h]
            l_s[h] = alpha[h] * l_s[h] + pv[HEAD_DIM:HEAD_DIM + 1, :]
            acc_s[h] = alpha[h] * acc_s[h] + pv[:HEAD_DIM, :]

    def step(q, fq, k_of, vt_of, kx_of):
        ut = [_dot_nt(jnp.concatenate([k_of(h), kx_of(h)], axis=1), jnp.concatenate([q[h], q_extra], axis=1))
              for h in heads]
        m_old = [m_s[h] for h in heads]
        m_new = [jnp.maximum(m_old[h], jnp.max(ut[h], axis=0, keepdims=True) + fq[h]) for h in heads]
        pt = [jnp.exp2(ut[h] - (m_new[h] - fq[h])) for h in heads]
        alpha = [jnp.exp2(m_old[h] - m_new[h]) for h in heads]
        for h in heads:
            pv = _dot(jnp.concatenate([vt_of(h), ones_rows], axis=0), pt[h].astype(BF16))
            m_s[h] = m_new[h]
            l_s[h] = alpha[h] * l_s[h] + pv[HEAD_DIM:HEAD_DIM + 1, :]
            acc_s[h] = alpha[h] * acc_s[h] + pv[:HEAD_DIM, :]

    for qi in range(nblk):
        qs = slice(qi * t, (qi + 1) * t)
        q = [q_ref[qs, hs(h)] for h in heads]
        fq = [f_ref[h, qi] * LOG2E for h in heads]
        m_s[...] = jnp.full(m_s.shape, NEG_INF, F32)
        l_s[...] = jnp.zeros(l_s.shape, F32)
        acc_s[...] = jnp.zeros(acc_s.shape, F32)

        def body(ki, carry, q=q, fq=fq):
            start = pl.multiple_of(ki * t, t)
            step(q, fq, lambda h: k_ref[pl.ds(start, t), hs(h)], lambda h: vt_ref[ki, hs(h), :],
                 lambda h: kx_s[h, ki])
            return carry

        lax.fori_loop(0, qi, body, 0)
        diag_step(q, fq, qi)
        for h in heads:
            o = jnp.transpose(acc_s[h] / l_s[h])
            og_ref[qs, hs(h)] = (o * sg_ref[qs, hs(h)]).astype(BF16)


def _prompt_attn(q16, k16, vt16, sg, fcum, bsz, s_len):
    t = ATTN_BLOCK
    hp = HEADS_PER_STEP
    nblk = s_len // t
    n, width = q16.shape
    blk = pl.BlockSpec((s_len, hp * HEAD_DIM), lambda b, g: (b, g))
    return pl.pallas_call(
        _prompt_attn_kernel,
        grid=(bsz, HEADS // hp),
        in_specs=[blk, blk,
                  pl.BlockSpec((nblk, hp * HEAD_DIM, t), lambda b, g: (b, g, 0)),
                  blk,
                  pl.BlockSpec((None, hp, nblk, 1, t), lambda b, g: (b, g, 0, 0, 0))],
        out_specs=blk,
        out_shape=jax.ShapeDtypeStruct((n, width), BF16),
        scratch_shapes=[pltpu.VMEM((hp, 1, t), F32), pltpu.VMEM((hp, 1, t), F32),
                        pltpu.VMEM((hp, HEAD_DIM, t), F32), pltpu.VMEM((hp, nblk, t, HEAD_DIM), BF16)],
        compiler_params=_params(2),
        name="prompt_attn",
    )(q16, k16, vt16, sg, fcum)


def _sample_attn_kernel(q_ref, ck_ref, cv_ref, kn_ref, vn_ref, sg_ref, fp_ref, fl_ref, fn_ref,
                        og_ref, m_ref, l_ref, acc_ref, fq_ref):
    ki = pl.program_id(1)
    last = pl.num_programs(1) - 1
    t = q_ref.shape[0]
    tk = fp_ref.shape[1]
    heads = range(HEADS)
    hs = lambda hd: slice(hd * HEAD_DIM, (hd + 1) * HEAD_DIM)
    rows = lax.broadcasted_iota(jnp.int32, (t, t), 0)
    cols = lax.broadcasted_iota(jnp.int32, (t, t), 1)

    @pl.when(ki == 0)
    def _():
        m_ref[...] = jnp.full(m_ref.shape, NEG_INF, F32)
        l_ref[...] = jnp.zeros(l_ref.shape, F32)
        acc_ref[...] = jnp.zeros(acc_ref.shape, F32)
        f_new = (fl_ref[...] + fn_ref[...]) * LOG2E
        for hd in heads:
            fq_ref[hd] = _row_to_col(f_new[hd:hd + 1, :])

    def update(with_new):
        f_new = (fl_ref[...] + fn_ref[...]) * LOG2E
        f_past = fp_ref[...] * LOG2E
        s = []
        for hd in heads:
            q = q_ref[:, hs(hd)]
            k = ck_ref[pl.ds(hd, tk, stride=HEADS), :].astype(BF16)
            segs = [_dot_nt(q, k) - f_past[hd:hd + 1, :]]
            if with_new:
                sn = _dot_nt(q, kn_ref[:, hs(hd)]) - f_new[hd:hd + 1, :]
                segs.append(jnp.where(cols <= rows, sn, NEG_INF))
            s.append(segs)
        fq = [fq_ref[hd] for hd in heads]
        m_old = [m_ref[hd] for hd in heads]
        m_new = [jnp.maximum(m_old[hd], functools.reduce(
            jnp.maximum, [jnp.max(x, axis=1, keepdims=True) for x in s[hd]]) + fq[hd]) for hd in heads]
        p = [[jnp.exp2(x - (m_new[hd] - fq[hd])) for x in s[hd]] for hd in heads]
        alpha = [jnp.exp2(m_old[hd] - m_new[hd]) for hd in heads]
        l_new = [alpha[hd] * l_ref[hd] + sum(jnp.sum(x, axis=1, keepdims=True) for x in p[hd]) for hd in heads]
        for hd in heads:
            v = cv_ref[pl.ds(hd, tk, stride=HEADS), :].astype(BF16)
            pv = _dot(p[hd][0].astype(BF16), v)
            if with_new:
                pv = pv + _dot(p[hd][1].astype(BF16), vn_ref[:, hs(hd)])
            acc = alpha[hd] * acc_ref[hd] + pv
            if with_new:
                og_ref[:, hs(hd)] = (acc / l_new[hd] * sg_ref[:, hs(hd)]).astype(BF16)
            else:
                m_ref[hd] = m_new[hd]
                l_ref[hd] = l_new[hd]
                acc_ref[hd] = acc

    @pl.when(ki != last)
    def _():
        update(False)

    @pl.when(ki == last)
    def _():
        update(True)


def _sample_attn(q16, cache_k, cache_v, layer, k16, v16, sg, f_past, f_last, f_new, bsz, t):
    n, width = q16.shape
    p_len = f_past.shape[2]
    tk = SAMPLE_KV_BLOCK
    new = pl.BlockSpec((t, width), lambda b, k: (b, 0))
    past = pl.BlockSpec((None, None, tk * HEADS, HEAD_DIM), lambda b, k: (layer, b, k, 0))
    return pl.pallas_call(
        _sample_attn_kernel,
        grid=(bsz, p_len // tk),
        in_specs=[new, past, past, new, new, new,
                  pl.BlockSpec((None, HEADS, tk), lambda b, k: (b, 0, k)),
                  pl.BlockSpec((None, HEADS, 1), lambda b, k: (b, 0, 0)),
                  pl.BlockSpec((None, HEADS, t), lambda b, k: (b, 0, 0))],
        out_specs=new,
        out_shape=jax.ShapeDtypeStruct((n, width), BF16),
        scratch_shapes=[pltpu.VMEM((HEADS, t, 1), F32), pltpu.VMEM((HEADS, t, 1), F32),
                        pltpu.VMEM((HEADS, t, HEAD_DIM), F32), pltpu.VMEM((HEADS, t, 1), F32)],
        compiler_params=_params(2),
        name="sample_attn",
    )(q16, cache_k, cache_v, k16, v16, sg, f_past, f_last, f_new)


def _gmlp_kernel(x_ref, ap_ref, wp_ref, g_ref, win_ref, vg_ref, mix_ref, bias_ref, wout_ref, *rest, emit_v):
    if emit_v:
        o_ref, vn_ref, a_ref = rest
    else:
        o_ref, a_ref = rest
    tm = x_ref.shape[0]
    bw = wout_ref.shape[0]
    gd = bw // GROUPS
    x = x_ref[...] + _dot(ap_ref[...], wp_ref[...])
    h = _rmsnorm(x, g_ref[...]).astype(BF16)
    v = _dot(h, win_ref[:, bw:bw + gd])
    for gi in range(GROUPS):
        sl = slice(gi * gd, (gi + 1) * gd)
        u = _dot(h, win_ref[:, sl])
        gate = _dot(h, win_ref[:, 2 * bw + gi * gd:2 * bw + (gi + 1) * gd])
        vn = _rmsnorm(v, vg_ref[:, sl])
        if gi + 1 < GROUPS:
            v = _dot(h, win_ref[:, bw + (gi + 1) * gd:bw + (gi + 2) * gd])
        if emit_v:
            vn_ref[:, sl] = vn
        vn16 = vn.astype(BF16)
        mixed = jnp.concatenate(
            [_dot(mix_ref[gi], vn16[ci * MLP_CHUNK:(ci + 1) * MLP_CHUNK, :]) + bias_ref[:, gi:gi + 1]
             for ci in range(tm // MLP_CHUNK)], axis=0)
        a_ref[:, sl] = (u * mixed * jax.nn.silu(gate)).astype(BF16)
    o_ref[...] = x + _dot(a_ref[...], wout_ref[...])


def _gmlp_layer(x, a_prev, w_prev, g, win, vg, mix, bias, wout, layer, emit_v):
    n, d = x.shape
    bw = wout.shape[1]
    tm = GMLP_ROW_TILE
    row = lambda cols: pl.BlockSpec((tm, cols), lambda i: (i, 0))
    out_shape = [jax.ShapeDtypeStruct((n, d), F32)]
    out_specs = [row(d)]
    if emit_v:
        out_shape.append(jax.ShapeDtypeStruct((n, bw), F32))
        out_specs.append(row(bw))
    res = pl.pallas_call(
        functools.partial(_gmlp_kernel, emit_v=emit_v),
        grid=(n // tm,),
        in_specs=[row(d), row(a_prev.shape[1]), _layer_spec(w_prev.shape, layer),
                  _const_spec((1, d)), _layer_spec(win.shape, layer), _const_spec((1, bw)),
                  _const_spec(mix.shape), _const_spec(bias.shape), _layer_spec(wout.shape, layer)],
        out_specs=tuple(out_specs),
        out_shape=tuple(out_shape),
        scratch_shapes=[pltpu.VMEM((tm, bw), BF16)],
        compiler_params=_params(1),
        name="gmlp_layer",
    )(x, a_prev, w_prev, g.reshape(1, d), win, vg.reshape(1, bw), mix, bias, wout)
    return res if emit_v else res[0]


def _sgu_mask():
    c = jnp.arange(MLP_CHUNK) // STREAM_CHUNK
    return c[None, :] <= c[:, None]


def kernel(x_prompt, x_sample, cache_k, cache_v, cache_logf, norm_g, w_in_a, b_f, q_g, k_g, w_out_a,
           w_in_b, v_g, ws, bs, w_out_b):
    bsz, s_len, d = x_prompt.shape
    dbsz, t_new, _ = x_sample.shape
    p_len = cache_k.shape[2]
    width = HEADS * HEAD_DIM
    depth = norm_g.shape[0]
    assert depth % 2 == 0
    assert ROW_TILE == ATTN_BLOCK and HEADS % HEADS_PER_STEP == 0
    assert s_len % ATTN_BLOCK == 0 and s_len % CUMSUM_BLOCK == 0 and p_len % CUMSUM_BLOCK == 0
    assert p_len % SAMPLE_KV_BLOCK == 0 and MLP_CHUNK == 2 * t_new and GMLP_ROW_TILE % MLP_CHUNK == 0

    xp = x_prompt.reshape(bsz * s_len, d)
    xs = x_sample.reshape(dbsz * t_new, d)
    n_a = w_in_a.shape[0]
    ck = cache_k.reshape(n_a, dbsz, p_len * HEADS, HEAD_DIM)
    cv = cache_v.reshape(n_a, dbsz, p_len * HEADS, HEAD_DIM)
    kv_p, kv_s = None, None
    lp_l, ls_l, sgu_l = [], [], []
    w = w_in_a.astype(BF16)
    wf = jnp.pad(w_in_a[:, :, 4 * width:], ((0, 0), (0, 0), (0, HEAD_DIM - HEADS))).astype(BF16)
    wo = w_out_a.astype(BF16)
    win = w_in_b.astype(BF16)
    wout = w_out_b.astype(BF16)
    bw = wout.shape[1]
    for i in range(depth):
        j = i // 2
        if i % 2 == 0:

            q16, k_all, v_all, k16, v16, sg, logf = _fox_inproj(xp, norm_g[i], w, wf, b_f[j], q_g[j], k_g[j],
                                                                j, n_a, kv_p, v_transposed=True)
            kv_p = (k_all, v_all)
            nb = s_len // CUMSUM_BLOCK
            lt = jnp.transpose(logf.reshape(bsz, s_len, HEADS), (0, 2, 1))
            fcum = _cumsum_lanes(lt.reshape(bsz, HEADS, nb, CUMSUM_BLOCK), chained=True)
            fcum = fcum.reshape(bsz, HEADS, s_len // ATTN_BLOCK, 1, ATTN_BLOCK)
            og_p = _prompt_attn(q16, k16, v16, sg, fcum, bsz, s_len)
            lp_l.append(logf.reshape(bsz, s_len, HEADS))

            q16, k_all, v_all, k16, v16, sg, logf = _fox_inproj(xs, norm_g[i], w, wf, b_f[j], q_g[j], k_g[j],
                                                                j, n_a, kv_s, v_transposed=False)
            kv_s = (k_all, v_all)
            nbp = p_len // CUMSUM_BLOCK
            cl = jnp.transpose(cache_logf[j], (0, 2, 1)).reshape(dbsz, HEADS, nbp, CUMSUM_BLOCK)
            f_past = _cumsum_lanes(cl, chained=True).reshape(dbsz, HEADS, p_len)
            lt = jnp.transpose(logf.reshape(dbsz, t_new, HEADS), (0, 2, 1))
            f_new = _cumsum_lanes(lt.reshape(1, HEADS, dbsz, t_new), chained=False)
            f_new = f_new.reshape(dbsz, HEADS, t_new)
            og_s = _sample_attn(q16, ck, cv, j, k16, v16, sg, f_past, f_past[:, :, p_len - 1:], f_new, dbsz, t_new)
            ls_l.append(logf.reshape(dbsz, t_new, HEADS))
        else:
            mix_p = (ws[j] * _sgu_mask()[None]).astype(BF16)
            bias_p = jnp.transpose(bs[j])
            xp = _gmlp_layer(xp, og_p, wo, norm_g[i], win, v_g[j], mix_p, bias_p, wout, j, emit_v=False)
            a = (ws[j] * _sgu_mask()[None])[:, :t_new, :t_new]
            z = jnp.zeros_like(a)
            mix_s = jnp.concatenate([jnp.concatenate([a, z], axis=2),
                                     jnp.concatenate([z, a], axis=2)], axis=1).astype(BF16)
            bias_s = jnp.transpose(jnp.concatenate([bs[j][:, :t_new], bs[j][:, :t_new]], axis=1))
            xs, sv = _gmlp_layer(xs, og_s, wo, norm_g[i], win, v_g[j], mix_s, bias_s, wout, j, emit_v=True)
            sgu_l.append(sv.reshape(dbsz, t_new, bw))
    kv5 = lambda a, nb_, t_: a.reshape(n_a, nb_, t_, HEADS, HEAD_DIM)
    return (xp.reshape(bsz, s_len, d), xs.reshape(dbsz, t_new, d),
            kv5(kv_p[0], bsz, s_len), kv5(kv_p[1], bsz, s_len), jnp.stack(lp_l),
            kv5(kv_s[0], dbsz, t_new), kv5(kv_s[1], dbsz, t_new), jnp.stack(ls_l), jnp.stack(sgu_l))
```

```python
import functools

import jax
import jax.numpy as jnp
from jax import lax
from jax.experimental import pallas as pl
from jax.experimental.pallas import tpu as pltpu

F32 = jnp.float32
BF16 = jnp.bfloat16

EPS = 1e-6
NEG_INF = -1e30
LOG2E = 1.4426950408889634

HEADS = 8
HEAD_DIM = 128
Q_PRESCALE = HEAD_DIM ** -0.5 * LOG2E
GROUPS = 8
MLP_CHUNK = 128
STREAM_CHUNK = 64

V7X_LANES = 128
V7X_SUBLANES = 8
BF16_SUBLANE_ROWS = 16
SPLIT_PARTS = 3
V7X_VMEM_BYTES = 64 * 1024 * 1024
VMEM_LIMIT_BYTES = V7X_VMEM_BYTES - 8 * 1024 * 1024

ROW_TILE = 512
GMLP_ROW_TILE = 512
ATTN_BLOCK = 512
HEADS_PER_STEP = 4
SAMPLE_KV_BLOCK = 2048
CUMSUM_BLOCK = 256
CUMSUM_ROWS = 512


def _params(n_axes):
    return pltpu.CompilerParams(dimension_semantics=("arbitrary",) * n_axes,
                                vmem_limit_bytes=VMEM_LIMIT_BYTES)


def _const_spec(shape):
    nd = len(shape)
    return pl.BlockSpec(shape, lambda *_: (0,) * nd, pipeline_mode=pl.Buffered(1))


def _layer_spec(stacked_shape, layer):
    nd = len(stacked_shape) - 1
    return pl.BlockSpec((None,) + tuple(stacked_shape[1:]), lambda *_: (layer,) + (0,) * nd,
                        pipeline_mode=pl.Buffered(1))


def _rmsnorm(x, g):
    ms = jnp.mean(x * x, axis=-1, keepdims=True)
    return x * lax.rsqrt(ms + EPS) * g


def _dot(a, b):
    return jnp.dot(a, b, preferred_element_type=F32)


def _dot_nt(a, b):
    return lax.dot_general(a, b, (((1,), (1,)), ((), ())), preferred_element_type=F32)


def _fox_inproj_kernel(x_ref, g_ref, w_ref, bf_ref, qg_ref, kg_ref, *rest, layer, first, v_transposed):
    q16_ref, k32_ref, v32_ref, k16_ref, v16_ref, sg_ref, logf_ref = rest[-7:]
    if first:
        for slot in range(k32_ref.shape[0]):
            if slot != layer:
                k32_ref[slot] = jnp.zeros(k32_ref.shape[1:], F32)
                v32_ref[slot] = jnp.zeros(v32_ref.shape[1:], F32)
        k32_ref = k32_ref.at[layer]
        v32_ref = v32_ref.at[layer]
    tm = x_ref.shape[0]
    width = HEADS * HEAD_DIM
    half = tm // 2
    for r0 in (0, half):
        rs = slice(r0, r0 + half)
        h = _rmsnorm(x_ref[rs, :], g_ref[...]).astype(BF16)

        zq = _dot(h, w_ref[:, 0:width])
        for hd in range(HEADS):
            sl = slice(hd * HEAD_DIM, (hd + 1) * HEAD_DIM)
            q16_ref[rs, sl] = (_rmsnorm(zq[:, sl], qg_ref[...]) * Q_PRESCALE).astype(BF16)

        zk = _dot(h, w_ref[:, width:2 * width])
        for hd in range(HEADS):
            sl = slice(hd * HEAD_DIM, (hd + 1) * HEAD_DIM)
            kn = _rmsnorm(zk[:, sl], kg_ref[...])
            k32_ref[pl.ds(r0 * HEADS + hd, half, stride=HEADS), :] = kn
            k16_ref[rs, sl] = kn.astype(BF16)

        zv = _dot(h, w_ref[:, 2 * width:3 * width])
        for hd in range(HEADS):
            sl = slice(hd * HEAD_DIM, (hd + 1) * HEAD_DIM)
            v32_ref[pl.ds(r0 * HEADS + hd, half, stride=HEADS), :] = zv[:, sl]
            if v_transposed:
                v16_ref[sl, rs] = jnp.transpose(zv[:, sl]).astype(BF16)
        if not v_transposed:
            v16_ref[rs, :] = zv.astype(BF16)

        zg = _dot(h, w_ref[:, 3 * width:4 * width])
        sg_ref[rs, :] = jax.nn.silu(zg)

        zf = _dot(h, w_ref[:, 4 * width:4 * width + HEADS])
        logf_ref[rs, :] = jax.nn.log_sigmoid(zf + bf_ref[...])


def _fox_inproj(x, g, w, b_f, q_g, k_g, layer, n_layers, kv_prev, v_transposed):
    n, d = x.shape
    width = HEADS * HEAD_DIM
    tm = ROW_TILE
    first = kv_prev is None
    row = lambda cols: pl.BlockSpec((tm, cols), lambda i: (i, 0))
    if first:
        kv_spec = pl.BlockSpec((n_layers, tm * HEADS, HEAD_DIM), lambda i: (0, i, 0))
    else:
        kv_spec = pl.BlockSpec((None, tm * HEADS, HEAD_DIM), lambda i: (layer, i, 0))
    kv_shape = jax.ShapeDtypeStruct((n_layers, n * HEADS, HEAD_DIM), F32)
    if v_transposed:
        v16_shape = jax.ShapeDtypeStruct((n // tm, width, tm), BF16)
        v16_spec = pl.BlockSpec((None, width, tm), lambda i: (i, 0, 0))
    else:
        v16_shape = jax.ShapeDtypeStruct((n, width), BF16)
        v16_spec = row(width)
    out_shape = (
        jax.ShapeDtypeStruct((n, width), BF16),
        kv_shape,
        kv_shape,
        jax.ShapeDtypeStruct((n, width), BF16),
        v16_shape,
        jax.ShapeDtypeStruct((n, width), F32),
        jax.ShapeDtypeStruct((n, HEADS), F32),
    )
    in_specs = [row(d), _const_spec((1, d)), _layer_spec(w.shape, layer),
                _const_spec((1, HEADS)), _const_spec((1, HEAD_DIM)), _const_spec((1, HEAD_DIM))]
    args = [x, g.reshape(1, d), w, b_f.reshape(1, HEADS), q_g.reshape(1, HEAD_DIM), k_g.reshape(1, HEAD_DIM)]
    aliases = {}
    if not first:
        in_specs += [pl.BlockSpec(memory_space=pl.ANY)] * 2
        aliases = {len(args): 1, len(args) + 1: 2}
        args += list(kv_prev)
    return pl.pallas_call(
        functools.partial(_fox_inproj_kernel, layer=layer, first=first, v_transposed=v_transposed),
        grid=(n // tm,),
        in_specs=in_specs,
        out_specs=(row(width), kv_spec, kv_spec, row(width), v16_spec, row(width), row(HEADS)),
        out_shape=out_shape,
        input_output_aliases=aliases,
        compiler_params=_params(1),
        name="fox_inproj",
    )(*args)


def _split3(x):
    hi = x.astype(BF16)
    r1 = x - hi.astype(F32)
    mid = r1.astype(BF16)
    lo = (r1 - mid.astype(F32)).astype(BF16)
    return hi, mid, lo


def _cumsum_kernel(x_ref, o_ref, *, chain_shift):
    rows, lb = x_ref.shape
    r = lax.broadcasted_iota(jnp.int32, (lb, lb), 0)
    c = lax.broadcasted_iota(jnp.int32, (lb, lb), 1)
    upper = (r <= c).astype(BF16)
    hi, mid, lo = _split3(x_ref[...])
    cs = _dot(hi, upper) + _dot(mid, upper) + _dot(lo, upper)
    if chain_shift:
        rb = lax.broadcasted_iota(jnp.int32, (rows, rows), 0)
        cb = lax.broadcasted_iota(jnp.int32, (rows, rows), 1)
        tot = jnp.broadcast_to(cs[:, lb - 1:lb], (rows, rows))
        tot_row = jnp.sum(jnp.where(rb == cb, tot, 0.0), axis=0, keepdims=True)
        earlier = (cb < rb) & (lax.shift_right_logical(cb, chain_shift) == lax.shift_right_logical(rb, chain_shift))
        offs = jnp.sum(jnp.where(earlier, jnp.broadcast_to(tot_row, (rows, rows)), 0.0),
                       axis=1, keepdims=True)
        cs = cs + offs
    o_ref[...] = cs


def _cumsum_lanes(x, chained):
    bsz, nh, nb, lb = x.shape
    chain_shift = (nb.bit_length() - 1) if chained else 0
    assert nb == 1 << (nb.bit_length() - 1)
    total = bsz * nh * nb
    rows = min(total, CUMSUM_ROWS)
    assert total % rows == 0 and rows % nb == 0
    spec = pl.BlockSpec((None, rows, lb), lambda b: (b, 0, 0))
    out = pl.pallas_call(
        functools.partial(_cumsum_kernel, chain_shift=chain_shift),
        grid=(total // rows,),
        in_specs=[spec],
        out_specs=spec,
        out_shape=jax.ShapeDtypeStruct((total // rows, rows, lb), F32),
        compiler_params=_params(1),
        name="logf_cumsum",
    )(x.reshape(total // rows, rows, lb))
    return out.reshape(x.shape)


def _row_to_col(row):
    n = row.shape[1]
    c = min(n, V7X_LANES)
    eye = lax.broadcasted_iota(jnp.int32, (c, c), 0) == lax.broadcasted_iota(jnp.int32, (c, c), 1)
    pieces = [jnp.sum(jnp.where(eye, jnp.broadcast_to(row[:, i:i + c], (c, c)), 0.0), axis=1, keepdims=True)
              for i in range(0, n, c)]
    return pieces[0] if len(pieces) == 1 else jnp.concatenate(pieces, axis=0)


def _prompt_attn_kernel(q_ref, k_ref, vt_ref, sg_ref, f_ref, og_ref, m_s, l_s, acc_s, kx_s):
    t = ATTN_BLOCK
    hp, nblk = f_ref.shape[0], f_ref.shape[1]
    heads = range(hp)
    hs = lambda h: slice(h * HEAD_DIM, (h + 1) * HEAD_DIM)
    key_i = lax.broadcasted_iota(jnp.int32, (t, t), 0)
    qry_i = lax.broadcasted_iota(jnp.int32, (t, t), 1)
    causal = key_i <= qry_i
    ones_rows = jnp.ones((BF16_SUBLANE_ROWS, t), BF16)
    sub_i = lax.broadcasted_iota(jnp.int32, (V7X_SUBLANES, t), 0)
    lane_i = lax.broadcasted_iota(jnp.int32, (t, HEAD_DIM), 1)
    q_extra = jnp.where(lane_i < SPLIT_PARTS, 1.0, 0.0).astype(BF16)
    for h in heads:
        for blk in range(nblk):
            hi, mid, lo = _split3(f_ref[h, blk] * -LOG2E)
            rows3 = jnp.where(sub_i == 0, hi.astype(F32),
                              jnp.where(sub_i == 1, mid.astype(F32), jnp.where(sub_i == 2, lo.astype(F32), 0.0)))
            padded = jnp.concatenate([rows3, jnp.zeros((HEAD_DIM - V7X_SUBLANES, t), F32)], axis=0)
            kx_s[h, blk] = jnp.transpose(padded).astype(BF16)

    def diag_step(q, fq, qi):
        h2 = t // 2
        lo, mid, hi = qi * t, qi * t + h2, (qi + 1) * t
        join = lambda a, b: jnp.concatenate([a[:, :h2], a[:, h2:] + b], axis=1)
        u1, u2 = [], []
        for h in heads:
            kx = kx_s[h, qi]
            qa = jnp.concatenate([q[h], q_extra], axis=1)
            x1 = _dot_nt(jnp.concatenate([k_ref[lo:mid, hs(h)], kx[:h2]], axis=1), qa)
            x2 = _dot_nt(jnp.concatenate([k_ref[mid:hi, hs(h)], kx[h2:]], axis=1), qa[h2:])
            u1.append(jnp.where(causal[:h2, :], x1, NEG_INF))
            u2.append(jnp.where(causal[:h2, :h2], x2, NEG_INF))
        m_old = [m_s[h] for h in heads]
        top = [jnp.max(u1[h], axis=0, keepdims=True) for h in heads]
        top = [jnp.concatenate([top[h][:, :h2], jnp.maximum(top[h][:, h2:], jnp.max(u2[h], axis=0, keepdims=True))],
                               axis=1) for h in heads]
        m_new = [jnp.maximum(m_old[h], top[h] + fq[h]) for h in heads]
        off = [m_new[h] - fq[h] for h in heads]
        p1 = [jnp.exp2(u1[h] - off[h]) for h in heads]
        p2 = [jnp.exp2(u2[h] - off[h][:, h2:]) for h in heads]
        alpha = [jnp.exp2(m_old[h] - m_new[h]) for h in heads]
        for h in heads:
            vt = vt_ref[qi, hs(h), :]
            pv1 = _dot(jnp.concatenate([vt[:, :h2], ones_rows[:, :h2]], axis=0), p1[h].astype(BF16))
            pv2 = _dot(jnp.concatenate([vt[:, h2:], ones_rows[:, :h2]], axis=0), p2[h].astype(BF16))
            pv = join(pv1, pv2)
            m_s[h] = m_new[h]
            l_s[h] = alpha[h] * l_s[h] + pv[HEAD_DIM:HEAD_DIM + 1, :]
            acc_s[h] = alpha[h] * acc_s[h] + pv[:HEAD_DIM, :]

    def step(q, fq, k_of, vt_of, kx_of):
        ut = [_dot_nt(jnp.concatenate([k_of(h), kx_of(h)], axis=1), jnp.concatenate([q[h], q_extra], axis=1))
              for h in heads]
        m_old = [m_s[h] for h in heads]
        m_new = [jnp.maximum(m_old[h], jnp.max(ut[h], axis=0, keepdims=True) + fq[h]) for h in heads]
        pt = [jnp.exp2(ut[h] - (m_new[h] - fq[h])) for h in heads]
        alpha = [jnp.exp2(m_old[h] - m_new[h]) for h in heads]
        for h in heads:
            pv = _dot(jnp.concatenate([vt_of(h), ones_rows], axis=0), pt[h].astype(BF16))
            m_s[h] = m_new[h]
            l_s[h] = alpha[h] * l_s[h] + pv[HEAD_DIM:HEAD_DIM + 1, :]
            acc_s[h] = alpha[h] * acc_s[h] + pv[:HEAD_DIM, :]

    for qi in range(nblk):
        qs = slice(qi * t, (qi + 1) * t)
        q = [q_ref[qs, hs(h)] for h in heads]
        fq = [f_ref[h, qi] * LOG2E for h in heads]
        m_s[...] = jnp.full(m_s.shape, NEG_INF, F32)
        l_s[...] = jnp.zeros(l_s.shape, F32)
        acc_s[...] = jnp.zeros(acc_s.shape, F32)

        def body(ki, carry, q=q, fq=fq):
            start = pl.multiple_of(ki * t, t)
            step(q, fq, lambda h: k_ref[pl.ds(start, t), hs(h)], lambda h: vt_ref[ki, hs(h), :],
                 lambda h: kx_s[h, ki])
            return carry

        lax.fori_loop(0, qi, body, 0)
        diag_step(q, fq, qi)
        for h in heads:
            o = jnp.transpose(acc_s[h] / l_s[h])
            og_ref[qs, hs(h)] = (o * sg_ref[qs, hs(h)]).astype(BF16)


def _prompt_attn(q16, k16, vt16, sg, fcum, bsz, s_len):
    t = ATTN_BLOCK
    hp = HEADS_PER_STEP
    nblk = s_len // t
    n, width = q16.shape
    blk = pl.BlockSpec((s_len, hp * HEAD_DIM), lambda b, g: (b, g))
    return pl.pallas_call(
        _prompt_attn_kernel,
        grid=(bsz, HEADS // hp),
        in_specs=[blk, blk,
                  pl.BlockSpec((nblk, hp * HEAD_DIM, t), lambda b, g: (b, g, 0)),
                  blk,
                  pl.BlockSpec((None, hp, nblk, 1, t), lambda b, g: (b, g, 0, 0, 0))],
        out_specs=blk,
        out_shape=jax.ShapeDtypeStruct((n, width), BF16),
        scratch_shapes=[pltpu.VMEM((hp, 1, t), F32), pltpu.VMEM((hp, 1, t), F32),
                        pltpu.VMEM((hp, HEAD_DIM, t), F32), pltpu.VMEM((hp, nblk, t, HEAD_DIM), BF16)],
        compiler_params=_params(2),
        name="prompt_attn",
    )(q16, k16, vt16, sg, fcum)


def _sample_attn_kernel(q_ref, ck_ref, cv_ref, kn_ref, vn_ref, sg_ref, fp_ref, fl_ref, fn_ref,
                        og_ref, m_ref, l_ref, acc_ref, fq_ref):
    ki = pl.program_id(1)
    last = pl.num_programs(1) - 1
    t = q_ref.shape[0]
    tk = fp_ref.shape[1]
    heads = range(HEADS)
    hs = lambda hd: slice(hd * HEAD_DIM, (hd + 1) * HEAD_DIM)
    rows = lax.broadcasted_iota(jnp.int32, (t, t), 0)
    cols = lax.broadcasted_iota(jnp.int32, (t, t), 1)

    @pl.when(ki == 0)
    def _():
        m_ref[...] = jnp.full(m_ref.shape, NEG_INF, F32)
        l_ref[...] = jnp.zeros(l_ref.shape, F32)
        acc_ref[...] = jnp.zeros(acc_ref.shape, F32)
        f_new = (fl_ref[...] + fn_ref[...]) * LOG2E
        for hd in heads:
            fq_ref[hd] = _row_to_col(f_new[hd:hd + 1, :])

    def update(with_new):
        f_new = (fl_ref[...] + fn_ref[...]) * LOG2E
        f_past = fp_ref[...] * LOG2E
        s = []
        for hd in heads:
            q = q_ref[:, hs(hd)]
            k = ck_ref[pl.ds(hd, tk, stride=HEADS), :].astype(BF16)
            segs = [_dot_nt(q, k) - f_past[hd:hd + 1, :]]
            if with_new:
                sn = _dot_nt(q, kn_ref[:, hs(hd)]) - f_new[hd:hd + 1, :]
                segs.append(jnp.where(cols <= rows, sn, NEG_INF))
            s.append(segs)
        fq = [fq_ref[hd] for hd in heads]
        m_old = [m_ref[hd] for hd in heads]
        m_new = [jnp.maximum(m_old[hd], functools.reduce(
            jnp.maximum, [jnp.max(x, axis=1, keepdims=True) for x in s[hd]]) + fq[hd]) for hd in heads]
        p = [[jnp.exp2(x - (m_new[hd] - fq[hd])) for x in s[hd]] for hd in heads]
        alpha = [jnp.exp2(m_old[hd] - m_new[hd]) for hd in heads]
        l_new = [alpha[hd] * l_ref[hd] + sum(jnp.sum(x, axis=1, keepdims=True) for x in p[hd]) for hd in heads]
        for hd in heads:
            v = cv_ref[pl.ds(hd, tk, stride=HEADS), :].astype(BF16)
            pv = _dot(p[hd][0].astype(BF16), v)
            if with_new:
                pv = pv + _dot(p[hd][1].astype(BF16), vn_ref[:, hs(hd)])
            acc = alpha[hd] * acc_ref[hd] + pv
            if with_new:
                og_ref[:, hs(hd)] = (acc / l_new[hd] * sg_ref[:, hs(hd)]).astype(BF16)
            else:
                m_ref[hd] = m_new[hd]
                l_ref[hd] = l_new[hd]
                acc_ref[hd] = acc

    @pl.when(ki != last)
    def _():
        update(False)

    @pl.when(ki == last)
    def _():
        update(True)


def _sample_attn(q16, cache_k, cache_v, layer, k16, v16, sg, f_past, f_last, f_new, bsz, t):
    n, width = q16.shape
    p_len = f_past.shape[2]
    tk = SAMPLE_KV_BLOCK
    new = pl.BlockSpec((t, width), lambda b, k: (b, 0))
    past = pl.BlockSpec((None, None, tk * HEADS, HEAD_DIM), lambda b, k: (layer, b, k, 0))
    return pl.pallas_call(
        _sample_attn_kernel,
        grid=(bsz, p_len // tk),
        in_specs=[new, past, past, new, new, new,
                  pl.BlockSpec((None, HEADS, tk), lambda b, k: (b, 0, k)),
                  pl.BlockSpec((None, HEADS, 1), lambda b, k: (b, 0, 0)),
                  pl.BlockSpec((None, HEADS, t), lambda b, k: (b, 0, 0))],
        out_specs=new,
        out_shape=jax.ShapeDtypeStruct((n, width), BF16),
        scratch_shapes=[pltpu.VMEM((HEADS, t, 1), F32), pltpu.VMEM((HEADS, t, 1), F32),
                        pltpu.VMEM((HEADS, t, HEAD_DIM), F32), pltpu.VMEM((HEADS, t, 1), F32)],
        compiler_params=_params(2),
        name="sample_attn",
    )(q16, cache_k, cache_v, k16, v16, sg, f_past, f_last, f_new)


def _gmlp_kernel(x_ref, ap_ref, wp_ref, g_ref, win_ref, vg_ref, mix_ref, bias_ref, wout_ref, *rest, emit_v):
    if emit_v:
        o_ref, vn_ref, a_ref = rest
    else:
        o_ref, a_ref = rest
    tm = x_ref.shape[0]
    bw = wout_ref.shape[0]
    gd = bw // GROUPS
    x = x_ref[...] + _dot(ap_ref[...], wp_ref[...])
    h = _rmsnorm(x, g_ref[...]).astype(BF16)
    v = _dot(h, win_ref[:, bw:bw + gd])
    for gi in range(GROUPS):
        sl = slice(gi * gd, (gi + 1) * gd)
        u = _dot(h, win_ref[:, sl])
        gate = _dot(h, win_ref[:, 2 * bw + gi * gd:2 * bw + (gi + 1) * gd])
        vn = _rmsnorm(v, vg_ref[:, sl])
        if gi + 1 < GROUPS:
            v = _dot(h, win_ref[:, bw + (gi + 1) * gd:bw + (gi + 2) * gd])
        if emit_v:
            vn_ref[:, sl] = vn
        vn16 = vn.astype(BF16)
        mixed = jnp.concatenate(
            [_dot(mix_ref[gi], vn16[ci * MLP_CHUNK:(ci + 1) * MLP_CHUNK, :]) + bias_ref[:, gi:gi + 1]
             for ci in range(tm // MLP_CHUNK)], axis=0)
        a_ref[:, sl] = (u * mixed * jax.nn.silu(gate)).astype(BF16)
    o_ref[...] = x + _dot(a_ref[...], wout_ref[...])


def _gmlp_layer(x, a_prev, w_prev, g, win, vg, mix, bias, wout, layer, emit_v):
    n, d = x.shape
    bw = wout.shape[1]
    tm = GMLP_ROW_TILE
    row = lambda cols: pl.BlockSpec((tm, cols), lambda i: (i, 0))
    out_shape = [jax.ShapeDtypeStruct((n, d), F32)]
    out_specs = [row(d)]
    if emit_v:
        out_shape.append(jax.ShapeDtypeStruct((n, bw), F32))
        out_specs.append(row(bw))
    res = pl.pallas_call(
        functools.partial(_gmlp_kernel, emit_v=emit_v),
        grid=(n // tm,),
        in_specs=[row(d), row(a_prev.shape[1]), _layer_spec(w_prev.shape, layer),
                  _const_spec((1, d)), _layer_spec(win.shape, layer), _const_spec((1, bw)),
                  _const_spec(mix.shape), _const_spec(bias.shape), _layer_spec(wout.shape, layer)],
        out_specs=tuple(out_specs),
        out_shape=tuple(out_shape),
        scratch_shapes=[pltpu.VMEM((tm, bw), BF16)],
        compiler_params=_params(1),
        name="gmlp_layer",
    )(x, a_prev, w_prev, g.reshape(1, d), win, vg.reshape(1, bw), mix, bias, wout)
    return res if emit_v else res[0]


def _sgu_mask():
    c = jnp.arange(MLP_CHUNK) // STREAM_CHUNK
    return c[None, :] <= c[:, None]


def kernel(x_prompt, x_sample, cache_k, cache_v, cache_logf, norm_g, w_in_a, b_f, q_g, k_g, w_out_a,
           w_in_b, v_g, ws, bs, w_out_b):
    bsz, s_len, d = x_prompt.shape
    dbsz, t_new, _ = x_sample.shape
    p_len = cache_k.shape[2]
    width = HEADS * HEAD_DIM
    depth = norm_g.shape[0]
    assert depth % 2 == 0
    assert ROW_TILE == ATTN_BLOCK and HEADS % HEADS_PER_STEP == 0
    assert s_len % ATTN_BLOCK == 0 and s_len % CUMSUM_BLOCK == 0 and p_len % CUMSUM_BLOCK == 0
    assert p_len % SAMPLE_KV_BLOCK == 0 and MLP_CHUNK == 2 * t_new and GMLP_ROW_TILE % MLP_CHUNK == 0

    xp = x_prompt.reshape(bsz * s_len, d)
    xs = x_sample.reshape(dbsz * t_new, d)
    n_a = w_in_a.shape[0]
    ck = cache_k.reshape(n_a, dbsz, p_len * HEADS, HEAD_DIM)
    cv = cache_v.reshape(n_a, dbsz, p_len * HEADS, HEAD_DIM)
    kv_p, kv_s = None, None
    lp_l, ls_l, sgu_l = [], [], []
    w = w_in_a.astype(BF16)
    wo = w_out_a.astype(BF16)
    win = w_in_b.astype(BF16)
    wout = w_out_b.astype(BF16)
    bw = wout.shape[1]
    for i in range(depth):
        j = i // 2
        if i % 2 == 0:

            q16, k_all, v_all, k16, v16, sg, logf = _fox_inproj(xp, norm_g[i], w, b_f[j], q_g[j], k_g[j],
                                                                j, n_a, kv_p, v_transposed=True)
            kv_p = (k_all, v_all)
            nb = s_len // CUMSUM_BLOCK
            lt = jnp.transpose(logf.reshape(bsz, s_len, HEADS), (0, 2, 1))
            fcum = _cumsum_lanes(lt.reshape(bsz, HEADS, nb, CUMSUM_BLOCK), chained=True)
            fcum = fcum.reshape(bsz, HEADS, s_len // ATTN_BLOCK, 1, ATTN_BLOCK)
            og_p = _prompt_attn(q16, k16, v16, sg, fcum, bsz, s_len)
            lp_l.append(logf.reshape(bsz, s_len, HEADS))

            q16, k_all, v_all, k16, v16, sg, logf = _fox_inproj(xs, norm_g[i], w, b_f[j], q_g[j], k_g[j],
                                                                j, n_a, kv_s, v_transposed=False)
            kv_s = (k_all, v_all)
            nbp = p_len // CUMSUM_BLOCK
            cl = jnp.transpose(cache_logf[j], (0, 2, 1)).reshape(dbsz, HEADS, nbp, CUMSUM_BLOCK)
            f_past = _cumsum_lanes(cl, chained=True).reshape(dbsz, HEADS, p_len)
            lt = jnp.transpose(logf.reshape(dbsz, t_new, HEADS), (0, 2, 1))
            f_new = _cumsum_lanes(lt.reshape(1, HEADS, dbsz, t_new), chained=False)
            f_new = f_new.reshape(dbsz, HEADS, t_new)
            og_s = _sample_attn(q16, ck, cv, j, k16, v16, sg, f_past, f_past[:, :, p_len - 1:], f_new, dbsz, t_new)
            ls_l.append(logf.reshape(dbsz, t_new, HEADS))
        else:
            mix_p = (ws[j] * _sgu_mask()[None]).astype(BF16)
            bias_p = jnp.transpose(bs[j])
            xp = _gmlp_layer(xp, og_p, wo, norm_g[i], win, v_g[j], mix_p, bias_p, wout, j, emit_v=False)
            a = (ws[j] * _sgu_mask()[None])[:, :t_new, :t_new]
            z = jnp.zeros_like(a)
            mix_s = jnp.concatenate([jnp.concatenate([a, z], axis=2),
                                     jnp.concatenate([z, a], axis=2)], axis=1).astype(BF16)
            bias_s = jnp.transpose(jnp.concatenate([bs[j][:, :t_new], bs[j][:, :t_new]], axis=1))
            xs, sv = _gmlp_layer(xs, og_s, wo, norm_g[i], win, v_g[j], mix_s, bias_s, wout, j, emit_v=True)
            sgu_l.append(sv.reshape(dbsz, t_new, bw))
    kv5 = lambda a, nb_, t_: a.reshape(n_a, nb_, t_, HEADS, HEAD_DIM)
    return (xp.reshape(bsz, s_len, d), xs.reshape(dbsz, t_new, d),
            kv5(kv_p[0], bsz, s_len), kv5(kv_p[1], bsz, s_len), jnp.stack(lp_l),
            kv5(kv_s[0], dbsz, t_new), kv5(kv_s[1], dbsz, t_new), jnp.stack(ls_l), jnp.stack(sgu_l))
```
